```python
import functools
import jax, jax.numpy as jnp
from jax import lax
import numpy as np

D_MODEL = 1024
BATCH = 4
SEQ = 4096
DEPTH = 2
DEC_BATCH = 32
DEC_SEQ = 4
PAST_LEN = 16384
PAGE_SIZE = 128

N_META = 16
CONV_K = 3
D_MIX = D_MODEL
GLA_WIDTH = D_MIX // 4
GLA_DV = 64
GLA_HEADS = GLA_WIDTH // GLA_DV
GLA_DK = GLA_DV // 2
GLA_GATE_RANK = 16
GLA_GATE_TAU = 16.0
GLA_CHUNK = 64
SCONV_WIDTH = D_MIX // 4
ATT_WIDTH = D_MIX - GLA_WIDTH - SCONV_WIDTH
HEAD_DIM = 64
ATT_HEADS = ATT_WIDTH // HEAD_DIM
ROT_DIM = HEAD_DIM // 4
ROPE_THETA = 500000.0
IDX_HEADS = 4
IDX_DIM = 64
TOPK_MAX = 256
Q_BLOCK = 128
D_FF = 2816
EPS = 1e-6
PROJ_SIZES = (GLA_HEADS * GLA_DK, GLA_HEADS * GLA_DK, GLA_WIDTH, GLA_GATE_RANK, GLA_WIDTH,
              SCONV_WIDTH, SCONV_WIDTH, SCONV_WIDTH,
              ATT_WIDTH, ATT_WIDTH, ATT_WIDTH, IDX_HEADS * IDX_DIM, IDX_DIM, IDX_HEADS)
D_PROJ = sum(PROJ_SIZES)

kernel_name = 'hymba_gla_sconv_dsa_convffn_step'


def rmsnorm(x, g):
    xf = x.astype(jnp.float32)
    y = xf * lax.rsqrt(jnp.mean(xf * xf, axis=-1, keepdims=True) + EPS)
    return (y * g.astype(jnp.float32)).astype(x.dtype)


def rope_partial(x, pos):
    half = ROT_DIM // 2
    inv_freq = ROPE_THETA ** (-jnp.arange(half, dtype=jnp.float32) / half)
    ang = pos.astype(jnp.float32)[:, None] * inv_freq[None, :]
    cos = jnp.cos(ang)[:, None, :]
    sin = jnp.sin(ang)[:, None, :]
    xr = x[..., :ROT_DIM].astype(jnp.float32)
    x1, x2 = xr[..., :half], xr[..., half:]
    rot = jnp.concatenate([x1 * cos - x2 * sin, x2 * cos + x1 * sin], axis=-1)
    return jnp.concatenate([rot.astype(x.dtype), x[..., ROT_DIM:]], axis=-1)


def causal_dwconv(u, prev, w):
    L = u.shape[1]
    up = jnp.concatenate([prev.astype(u.dtype), u], axis=1)
    y = w[0] * up[:, 0:L]
    for i in range(1, CONV_K):
        y = y + w[i] * up[:, i:i + L]
    return y, up[:, L:]


def gla_chunked(q, k, v, g, s0, chunk):
    B, L, H, _ = q.shape
    n = L // chunk

    def blocks(a):
        a = a.astype(jnp.float32)
        return a.reshape(B, n, chunk, H, a.shape[-1]).transpose(1, 0, 3, 2, 4)

    causal = jnp.tril(jnp.ones((chunk, chunk), dtype=bool))

    def step(S, inp):
        qi, ki, vi, gi = inp
        b = jnp.cumsum(gi, axis=2)
        o_inter = jnp.einsum('bhtk,bhkv->bhtv', qi * jnp.exp(b), S)
        diff = b[:, :, :, None, :] - b[:, :, None, :, :]
        decay = jnp.exp(jnp.where(causal[:, :, None], diff, -jnp.inf))
        att = jnp.einsum('bhtk,bhsk,bhtsk->bhts', qi, ki, decay)
        o_intra = jnp.einsum('bhts,bhsv->bhtv', att, vi)
        b_last = b[:, :, -1:, :]
        S_new = jnp.exp(b_last[:, :, 0, :])[..., None] * S + jnp.einsum(
            'bhsk,bhsv->bhkv', ki * jnp.exp(b_last - b), vi)
        return S_new, o_inter + o_intra

    S, o = lax.scan(step, s0.astype(jnp.float32), (blocks(q), blocks(k), blocks(v), blocks(g)))
    o = o.transpose(1, 0, 3, 2, 4).reshape(B, L, H, -1)
    return o, S


def indexer_scores(iq, ik, w):
    s = jnp.einsum('bqhd,bld->bqhl', iq.astype(jnp.float32), ik.astype(jnp.float32)) * IDX_DIM ** -0.5
    return jnp.einsum('bqhl,bqh->bql', jax.nn.relu(s), w.astype(jnp.float32))


def select_keys(scores, q_pos, topk):
    L = scores.shape[-1]
    admissible = jnp.arange(L)[None, None, :] <= q_pos[None, :, None]
    _, idx = lax.top_k(jnp.where(admissible, scores, -jnp.inf), topk)
    return idx, idx <= q_pos[None, :, None]


def sparse_attend(q, k_sel, v_sel, valid):
    s = jnp.einsum('bqhd,bqkhd->bqhk', q, k_sel, preferred_element_type=jnp.float32) * HEAD_DIM ** -0.5
    s = jnp.where(valid[:, :, None, :], s, -jnp.inf)
    p = jax.nn.softmax(s, axis=-1).astype(v_sel.dtype)
    return jnp.einsum('bqhk,bqkhd->bqhd', p, v_sel)


def dsa_prompt(q, k, v, iq, ik, w_idx, topk):
    B, T, H, Dh = q.shape
    nb = -(-T // Q_BLOCK)
    Tp = nb * Q_BLOCK

    def blocks(a):
        a = jnp.pad(a, [(0, 0), (0, Tp - T)] + [(0, 0)] * (a.ndim - 2))
        return jnp.moveaxis(a.reshape((B, nb, Q_BLOCK) + a.shape[2:]), 1, 0)

    bix = jnp.arange(B)[:, None, None]

    def one_block(args):
        qb, iqb, wb, start = args
        q_pos = start + jnp.arange(Q_BLOCK)
        idx, valid = select_keys(indexer_scores(iqb, ik, wb), q_pos, topk)
        return sparse_attend(qb, k[bix, idx], v[bix, idx], valid)

    out = lax.map(one_block, (blocks(q), blocks(iq), blocks(w_idx), jnp.arange(nb) * Q_BLOCK))
    return jnp.moveaxis(out, 0, 1).reshape(B, Tp, H, Dh)[:, :T]


def dsa_sample(q, k_new, v_new, iq, ik_new, w_idx, cache_k, cache_v, cache_idx_k, page_table, layer, topk):
    DB, S = q.shape[:2]
    past = page_table.shape[1] * PAGE_SIZE
    ik_past = cache_idx_k[layer, page_table].reshape(DB, past, IDX_DIM)
    ik_all = jnp.concatenate([ik_past.astype(ik_new.dtype), ik_new], axis=1)
    q_pos = past + jnp.arange(S)
    idx, valid = select_keys(indexer_scores(iq, ik_all, w_idx), q_pos, topk)
    bix = jnp.arange(DB)[:, None, None]
    in_past = (idx < past)[..., None, None]
    pidx = jnp.minimum(idx, past - 1)
    phys = page_table[bix, pidx // PAGE_SIZE]
    off = pidx % PAGE_SIZE
    nidx = jnp.clip(idx - past, 0, S - 1)
    k_sel = jnp.where(in_past, cache_k[layer, phys, off].astype(k_new.dtype), k_new[bix, nidx])
    v_sel = jnp.where(in_past, cache_v[layer, phys, off].astype(v_new.dtype), v_new[bix, nidx])
    return sparse_attend(q, k_sel, v_sel, valid)


def layer_forward(h, pos, attn_fn, gla_s0, sconv_prev, ffn_prev, gla_chunk, gla_pad,
                  attn_norm, w_in, w_gate_up, b_gate, gla_norm, w_sconv, w_out,
                  ffn_norm, w_up, w_ffn_conv, w_down):
    B, L, _ = h.shape
    hn = rmsnorm(h, attn_norm)
    z = jnp.einsum('bld,de->ble', hn, w_in)
    cuts = [int(c) for c in np.cumsum(PROJ_SIZES)[:-1]]
    gq, gk, gv, g_lr, gr, sb, sc, sx, aq, ak, av, iq, ik, iw = jnp.split(z, cuts, axis=-1)

    q = gq.reshape(B, L, GLA_HEADS, GLA_DK) * GLA_DK ** -0.5
    k = gk.reshape(B, L, GLA_HEADS, GLA_DK)
    v = gv.reshape(B, L, GLA_HEADS, GLA_DV)
    g = jax.nn.log_sigmoid((jnp.einsum('blr,re->ble', g_lr, w_gate_up) + b_gate).astype(jnp.float32)) / GLA_GATE_TAU
    g = g.reshape(B, L, GLA_HEADS, GLA_DK)
    pad_back = (-(L + gla_pad)) % gla_chunk
    padw = ((0, 0), (gla_pad, pad_back), (0, 0), (0, 0))
    o, gla_state = gla_chunked(jnp.pad(q, padw), jnp.pad(k, padw), jnp.pad(v, padw), jnp.pad(g, padw),
                               gla_s0, gla_chunk)
    o = o[:, gla_pad:gla_pad + L]
    o_gla = (rmsnorm(o, gla_norm).reshape(B, L, GLA_WIDTH) * jax.nn.silu(gr.astype(jnp.float32))).astype(h.dtype)

    y_conv, sconv_buf = causal_dwconv(sc * sx, sconv_prev, w_sconv)
    o_sconv = sb * y_conv

    qa = rope_partial(aq.reshape(B, L, ATT_HEADS, HEAD_DIM), pos)
    ka = rope_partial(ak.reshape(B, L, ATT_HEADS, HEAD_DIM), pos)
    va = av.reshape(B, L, ATT_HEADS, HEAD_DIM)
    iqr = rope_partial(iq.reshape(B, L, IDX_HEADS, IDX_DIM), pos)
    ikr = rope_partial(ik[:, :, None, :], pos)[:, :, 0, :]
    o_att = attn_fn(qa, ka, va, iqr, ikr, iw * IDX_HEADS ** -0.5).reshape(B, L, ATT_WIDTH)

    mix = jnp.concatenate([o_gla, o_sconv, o_att], axis=-1)
    h = h + jnp.einsum('ble,ed->bld', mix, w_out)

    hn2 = rmsnorm(h, ffn_norm)
    a, b = jnp.split(jnp.einsum('bld,df->blf', hn2, w_up), 2, axis=-1)
    a_c, ffn_buf = causal_dwconv(a, ffn_prev, w_ffn_conv)
    h = h + jnp.einsum('blf,fd->bld', jax.nn.silu(a_c) * b, w_down)
    return h, (ka, va, ikr, gla_state, sconv_buf, ffn_buf)


def setup_inputs(seed: int = 0) -> dict:
    key = jax.random.key(seed)
    keys = iter(jax.random.split(key, 32))

    def nrm(shape, scale):
        return jax.random.normal(next(keys), shape, jnp.float32) * scale

    n_pages = PAST_LEN // PAGE_SIZE
    used = DEC_BATCH * n_pages
    n_pool = used + max(1, used // 4)
    page_table = jax.random.permutation(next(keys), n_pool)[:used].reshape(DEC_BATCH, n_pages).astype(jnp.int32)
    return {
        'x_prompt': nrm((BATCH, SEQ, D_MODEL), 1.0),
        'x_sample': nrm((DEC_BATCH, DEC_SEQ, D_MODEL), 1.0),
        'cache_k': nrm((DEPTH, n_pool, PAGE_SIZE, ATT_HEADS, HEAD_DIM), 1.0),
        'cache_v': nrm((DEPTH, n_pool, PAGE_SIZE, ATT_HEADS, HEAD_DIM), 1.0),
        'cache_idx_k': nrm((DEPTH, n_pool, PAGE_SIZE, IDX_DIM), 1.0),
        'page_table': page_table,
        'state_gla': nrm((DEPTH, DEC_BATCH, GLA_HEADS, GLA_DK, GLA_DV), 2.0),
        'state_sconv': nrm((DEPTH, DEC_BATCH, CONV_K - 1, SCONV_WIDTH), 1.0),
        'state_ffn_conv': nrm((DEPTH, DEC_BATCH, CONV_K - 1, D_FF), 1.0),
        'meta_tokens': nrm((N_META, D_MODEL), 1.0),
        'attn_norm': 1.0 + nrm((DEPTH, D_MODEL), 0.01),
        'w_in': nrm((DEPTH, D_MODEL, D_PROJ), D_MODEL ** -0.5),
        'w_gate_up': nrm((DEPTH, GLA_GATE_RANK, GLA_HEADS * GLA_DK), GLA_GATE_RANK ** -0.5),
        'b_gate': nrm((DEPTH, GLA_HEADS * GLA_DK), 0.01),
        'gla_norm': 1.0 + nrm((DEPTH, GLA_DV), 0.01),
        'w_sconv': nrm((DEPTH, CONV_K, SCONV_WIDTH), CONV_K ** -0.5),
        'w_out': nrm((DEPTH, D_MIX, D_MODEL), D_MIX ** -0.5),
        'ffn_norm': 1.0 + nrm((DEPTH, D_MODEL), 0.01),
        'w_up': nrm((DEPTH, D_MODEL, 2 * D_FF), D_MODEL ** -0.5),
        'w_ffn_conv': nrm((DEPTH, CONV_K, D_FF), CONV_K ** -0.5),
        'w_down': nrm((DEPTH, D_FF, D_MODEL), D_FF ** -0.5),
        'final_norm': 1.0 + nrm((D_MODEL,), 0.01),
    }


def reference(x_prompt, x_sample, cache_k, cache_v, cache_idx_k, page_table, state_gla, state_sconv,
              state_ffn_conv, meta_tokens, attn_norm, w_in, w_gate_up, b_gate, gla_norm, w_sconv, w_out,
              ffn_norm, w_up, w_ffn_conv, w_down, final_norm):
    B, S_len, D = x_prompt.shape
    T = S_len + N_META
    hp = jnp.concatenate([jnp.broadcast_to(meta_tokens[None].astype(x_prompt.dtype), (B, N_META, D)), x_prompt], axis=1)
    pos_p = jnp.arange(T)
    DB, S = x_sample.shape[:2]
    past = page_table.shape[1] * PAGE_SIZE
    pos_s = past + jnp.arange(S)
    topk_p = min(TOPK_MAX, T // 4)
    topk_s = min(TOPK_MAX, (past + S) // 4)
    gla_pad = (-N_META) % GLA_CHUNK
    hs = x_sample
    prompt_states = [[] for _ in range(6)]
    sample_states = [[] for _ in range(6)]
    for l in range(DEPTH):
        weights = (attn_norm[l], w_in[l], w_gate_up[l], b_gate[l], gla_norm[l], w_sconv[l], w_out[l],
                   ffn_norm[l], w_up[l], w_ffn_conv[l], w_down[l])
        hp, st_p = layer_forward(
            hp, pos_p, functools.partial(dsa_prompt, topk=topk_p),
            jnp.zeros((B, GLA_HEADS, GLA_DK, GLA_DV), jnp.float32),
            jnp.zeros((B, CONV_K - 1, SCONV_WIDTH), hp.dtype),
            jnp.zeros((B, CONV_K - 1, D_FF), hp.dtype),
            GLA_CHUNK, gla_pad, *weights)
        hs, st_s = layer_forward(
            hs, pos_s,
            functools.partial(dsa_sample, cache_k=cache_k, cache_v=cache_v, cache_idx_k=cache_idx_k,
                              page_table=page_table, layer=l, topk=topk_s),
            state_gla[l], state_sconv[l], state_ffn_conv[l], S, 0, *weights)
        for i in range(6):
            prompt_states[i].append(st_p[i])
            sample_states[i].append(st_s[i])
    y_prompt = rmsnorm(hp, final_norm)[:, N_META:]
    y_sample = rmsnorm(hs, final_norm)
    k_p, v_p, ik_p, gla_p, sconv_p, ffn_p = [jnp.stack(a, axis=0) for a in prompt_states]
    k_s, v_s, ik_s, gla_s, sconv_s, ffn_s = [jnp.stack(a, axis=0) for a in sample_states]
    return (y_prompt, y_sample, k_p, v_p, ik_p, gla_p, sconv_p, ffn_p, k_s, v_s, ik_s, gla_s, sconv_s, ffn_s)
```

```python
import functools

import jax
import jax.numpy as jnp
from jax import lax
from jax.experimental import pallas as pl
from jax.experimental.pallas import tpu as pltpu

F32 = jnp.float32
BF16 = jnp.bfloat16
I32 = jnp.int32

N_META = 16
GLA_HEADS, GLA_DK, GLA_DV = 4, 32, 64
GLA_RANK = 16
GLA_TAU = 16.0
GLA_W = GLA_HEADS * GLA_DV
GLA_QK = GLA_HEADS * GLA_DK
SCONV_W = 256
ATT_HEADS, HEAD_DIM = 8, 64
ATT_W = ATT_HEADS * HEAD_DIM
ROT_DIM = 16
ROPE_THETA = 500000.0
IDX_HEADS, IDX_DIM = 4, 64
IDX_W = IDX_HEADS * IDX_DIM
TOPK_MAX = 256
PAGE = 128
EPS = 1e-6
FF_CHUNK = 256

LANE = 128
QB = 128
GLA_CHUNK = 64
NEG_BIG = -1e30
VMEM_LIMIT = 56 * 1024 * 1024

SEG_GLA = (0, 768)
SEG_SCV = (768, 1536)
SEG_AQ = (1536, 2048)
SEG_AK = (2048, 2560)
SEG_AV = (2560, 3072)
SEG_IQ = (3072, 3328)
SEG_MISC = (3328, 3456)
N_PROJ = 3456
MISC_GLR = 64
MISC_IW = 80

INT_MIN = -2147483648
NEG_INF_KEY = -2139095041


def _tile(n, cands=(512, 384, 256, 128, 64, 32, 16, 8)):
    for c in cands:
        if n % c == 0:
            return c
    raise ValueError(f"no tile for {n}")


def _dot(a, b):
    return jnp.dot(a, b, preferred_element_type=F32)


def _dot_nt(a, b):
    return lax.dot_general(a, b, (((1,), (1,)), ((), ())), preferred_element_type=F32)


def _dot_tn(a, b):
    return lax.dot_general(a, b, (((0,), (0,)), ((), ())), preferred_element_type=F32)


def _cparams(n_axes):
    return pltpu.CompilerParams(dimension_semantics=("arbitrary",) * n_axes, vmem_limit_bytes=VMEM_LIMIT)


def _const_spec(shape):
    nd = len(shape)
    return pl.BlockSpec(shape, lambda *_: (0,) * nd, pipeline_mode=pl.Buffered(1))


def _to_key(x):
    bits = pltpu.bitcast(x, I32)
    return bits ^ ((bits >> 31) & 0x7FFFFFFF)


def _proj_kernel(x_ref, nw_ref, w_ref, wg_ref, bg_ref, rope_ref,
                 gla_ref, g_ref, scv_ref, q_ref, k_ref, v_ref, kb_ref, vb_ref, iq_ref, misc_ref, ikb_ref):
    x = x_ref[...]
    xn = x * lax.rsqrt(jnp.mean(x * x, axis=-1, keepdims=True) + EPS) * nw_ref[...]
    xb = xn.astype(BF16)

    def seg(s):
        return _dot(xb, w_ref[:, s[0]:s[1]])

    cos, s1, s2 = rope_ref[0], rope_ref[1], rope_ref[2]

    def rope128(z, c, a, b):
        return z * c + pltpu.roll(z, LANE - ROT_DIM // 2, 1) * a + pltpu.roll(z, ROT_DIM // 2, 1) * b

    gla_ref[...] = seg(SEG_GLA)
    scv_ref[...] = seg(SEG_SCV)

    zq = seg(SEG_AQ)
    zk = seg(SEG_AK)
    for c in range(ATT_W // LANE):
        sl = slice(c * LANE, (c + 1) * LANE)
        q_ref[:, sl] = (rope128(zq[:, sl], cos, s1, s2) * HEAD_DIM ** -0.5).astype(BF16)
        kr = rope128(zk[:, sl], cos, s1, s2)
        k_ref[:, sl] = kr
        kb_ref[:, sl] = kr.astype(BF16)
    zv = seg(SEG_AV)
    v_ref[...] = zv
    vb_ref[...] = zv.astype(BF16)

    zi = seg(SEG_IQ)
    for c in range(IDX_W // LANE):
        sl = slice(c * LANE, (c + 1) * LANE)
        iq_ref[:, sl] = (rope128(zi[:, sl], cos, s1, s2) * IDX_DIM ** -0.5).astype(BF16)

    zm = seg(SEG_MISC)
    gpre = jnp.dot(zm, wg_ref[...], preferred_element_type=F32, precision=lax.Precision.HIGHEST) + bg_ref[...]
    g_ref[...] = jax.nn.log_sigmoid(gpre) * (1.0 / GLA_TAU)
    lane = lax.broadcasted_iota(I32, zm.shape, 1)
    is_ik = lane < IDX_DIM
    mr = rope128(zm, jnp.where(is_ik, cos, 1.0), jnp.where(is_ik, s1, 0.0), jnp.where(is_ik, s2, 0.0))
    mr = mr * jnp.where((lane >= MISC_IW) & (lane < MISC_IW + IDX_HEADS), IDX_HEADS ** -0.5, 1.0)
    misc_ref[...] = mr
    ikb_ref[...] = mr[:, :IDX_DIM].astype(BF16)


def _proj(x, nw, w, wg, bg, rope):
    B, R, D = x.shape
    tm = _tile(R)
    grid = (B, R // tm)

    def row(width, dt=F32):
        return jax.ShapeDtypeStruct((B, R, width), dt)

    def rspec(width):
        return pl.BlockSpec((None, tm, width), lambda b, t: (b, t, 0))

    outs = [(768, F32), (GLA_QK, F32), (768, F32), (ATT_W, BF16), (ATT_W, F32), (ATT_W, F32), (ATT_W, BF16),
            (ATT_W, BF16), (IDX_W, BF16), (LANE, F32), (IDX_DIM, BF16)]
    return pl.pallas_call(
        _proj_kernel,
        grid=grid,
        in_specs=[rspec(D), _const_spec((1, D)), _const_spec((D, N_PROJ)), _const_spec((LANE, GLA_QK)),
                  _const_spec((1, GLA_QK)), pl.BlockSpec((3, tm, LANE), lambda b, t: (0, t, 0))],
        out_specs=[rspec(wd) for wd, _ in outs],
        out_shape=[row(wd, dt) for wd, dt in outs],
        compiler_params=_cparams(2),
        name="proj",
    )(x, nw, w, wg, bg, rope)


def _gla_kernel(gla_ref, g_ref, gn_ref, o_ref, st_ref, s_sc):
    t = pl.program_id(1)
    C = GLA_CHUNK

    @pl.when(t == 0)
    def _():
        s_sc[...] = jnp.zeros_like(s_sc)

    z = gla_ref[...]
    g = g_ref[...]
    r_i = lax.broadcasted_iota(I32, (C, C), 0)
    c_i = lax.broadcasted_iota(I32, (C, C), 1)
    causal = c_i <= r_i
    tri = jnp.where(causal, 1.0, 0.0).astype(F32)
    b = jnp.dot(tri, g, preferred_element_type=F32, precision=lax.Precision.HIGHEST)
    b_last = b[C - 1:C, :]
    b_mid = b[C // 2:C // 2 + 1, :]
    e_in = jnp.exp(b)
    e_q = jnp.exp(b - b_mid)
    e_k = jnp.exp(b_mid - b)
    e_out = jnp.exp(b_last - b)
    e_last = jnp.exp(b_last)
    qz = z[:, 0:GLA_QK] * GLA_DK ** -0.5
    kz = z[:, GLA_QK:2 * GLA_QK]
    q_in = (qz * e_in).astype(BF16)
    q_x = (qz * e_q).astype(BF16)
    k_x = (kz * e_k).astype(BF16)
    k_out = (kz * e_out).astype(BF16)
    gn = gn_ref[...]
    for h in range(GLA_HEADS):
        ks = slice(h * GLA_DK, (h + 1) * GLA_DK)
        v = z[:, 2 * GLA_QK + h * GLA_DV:2 * GLA_QK + (h + 1) * GLA_DV]
        gr = z[:, 2 * GLA_QK + GLA_W + h * GLA_DV:2 * GLA_QK + GLA_W + (h + 1) * GLA_DV]
        vb = v.astype(BF16)
        st = s_sc[h]
        att = jnp.where(causal, _dot_nt(q_x[:, ks], k_x[:, ks]), 0.0)
        o = _dot(att.astype(BF16), vb) + _dot_nt(q_in[:, ks], st.astype(BF16))
        s_sc[h] = st * e_last[:, ks] + _dot_tn(vb, k_out[:, ks])
        on = o * lax.rsqrt(jnp.mean(o * o, axis=-1, keepdims=True) + EPS) * gn
        o_ref[:, h * GLA_DV:(h + 1) * GLA_DV] = on * jax.nn.silu(gr)
    st_ref[...] = s_sc[...]


def _gla_prompt(gla, g, gn):
    B, R, _ = gla.shape
    C = GLA_CHUNK
    return pl.pallas_call(
        _gla_kernel,
        grid=(B, R // C),
        in_specs=[pl.BlockSpec((None, C, 768), lambda b, t: (b, t, 0)),
                  pl.BlockSpec((None, C, GLA_QK), lambda b, t: (b, t, 0)),
                  _const_spec((1, GLA_DV))],
        out_specs=[pl.BlockSpec((None, C, GLA_W), lambda b, t: (b, t, 0)),
                   pl.BlockSpec((None, GLA_HEADS, GLA_DV, GLA_DK), lambda b, t: (b, 0, 0, 0))],
        out_shape=[jax.ShapeDtypeStruct((B, R, GLA_W), F32),
                   jax.ShapeDtypeStruct((B, GLA_HEADS, GLA_DV, GLA_DK), F32)],
        scratch_shapes=[pltpu.VMEM((GLA_HEADS, GLA_DV, GLA_DK), F32)],
        compiler_params=_cparams(2),
        name="gla_prompt",
    )(gla, g, gn)


def _gla_step_kernel(qT_ref, kT_ref, gT_ref, vx_ref, gr_ref, gn_ref, s0_ref, o_ref, s_ref, *, n_tok):
    S = s0_ref[...]
    qT, kT, eT = qT_ref[...] * GLA_DK ** -0.5, kT_ref[...], jnp.exp(gT_ref[...])
    gn = gn_ref[...]
    gr = gr_ref[...]
    for t in range(n_tok):
        S = eT[:, t:t + 1] * S + kT[:, t:t + 1] * vx_ref[t]
        qs = qT[:, t:t + 1] * S
        for h in range(GLA_HEADS):
            o = jnp.sum(qs[h * GLA_DK:(h + 1) * GLA_DK, :], axis=0, keepdims=True)
            on = o * lax.rsqrt(jnp.mean(o * o, axis=-1, keepdims=True) + EPS) * gn
            hs = slice(h * GLA_DV, (h + 1) * GLA_DV)
            o_ref[t:t + 1, hs] = on * jax.nn.silu(gr[t:t + 1, hs])
    s_ref[...] = S


def _gla_sample(qT, kT, eT, vx, gr, gn, s0):
    DB, HK, n_tok = qT.shape

    def bspec(*shape):
        nd = len(shape)
        return pl.BlockSpec((None,) + shape, lambda b: (b,) + (0,) * nd)

    return pl.pallas_call(
        functools.partial(_gla_step_kernel, n_tok=n_tok),
        grid=(DB,),
        in_specs=[bspec(HK, n_tok), bspec(HK, n_tok), bspec(HK, n_tok), bspec(n_tok, HK, GLA_DV),
                  bspec(n_tok, GLA_W), _const_spec((1, GLA_DV)), bspec(HK, GLA_DV)],
        out_specs=[bspec(n_tok, GLA_W), bspec(HK, GLA_DV)],
        out_shape=[jax.ShapeDtypeStruct((DB, n_tok, GLA_W), F32), jax.ShapeDtypeStruct((DB, HK, GLA_DV), F32)],
        compiler_params=_cparams(1),
        name="gla_sample",
    )(qT, kT, eT, vx, gr, gn, s0)


def _count_cols(key_ref, n_chunks, cw, pred):
    rows = key_ref.shape[0]

    def body(c, acc):
        off = pl.multiple_of(c * cw, cw)
        k = key_ref[:, pl.ds(off, cw)]
        col = off + lax.broadcasted_iota(I32, (rows, cw), 1)
        m = jnp.where(pred(k, col), 1, 0)
        for j in range(cw // LANE):
            acc = acc + m[:, j * LANE:(j + 1) * LANE]
        return acc

    acc = lax.fori_loop(0, n_chunks, body, jnp.zeros((rows, LANE), I32))
    return jnp.sum(acc, axis=1, keepdims=True)


def _select_threshold(key_ref, js_ref, n_chunks, cw, topk, idx_bits):
    rows = key_ref.shape[0]

    def count(pred):
        return _count_cols(key_ref, n_chunks, cw, pred)

    c0 = count(lambda k, col: k >= 0)
    lo = jnp.where(c0 >= topk, 0, INT_MIN).astype(I32)

    def bs(it, lo):
        cand = lo + lax.shift_left(jnp.int32(1), 30 - it)
        c = count(lambda k, col: k >= cand)
        return jnp.where(c >= topk, cand, lo)

    thr = lax.fori_loop(0, 31, bs, lo)
    n_gt = count(lambda k, col: k > thr)
    n_ge = count(lambda k, col: k >= thr)
    need = topk - n_gt
    excess = (n_ge > topk) & (thr > NEG_INF_KEY)
    big = jnp.int32(1 << idx_bits)
    js_ref[...] = jnp.full(js_ref.shape, big, I32)

    @pl.when(jnp.max(jnp.where(excess, 1, 0)) > 0)
    def _():
        def bs2(it, j):
            cand = j + lax.shift_left(jnp.int32(1), idx_bits - 1 - it)
            c = count(lambda k, col: (k == thr) & (col < cand))
            return jnp.where(c < need, cand, j)

        j = lax.fori_loop(0, idx_bits, bs2, jnp.zeros((rows, 1), I32))
        js_ref[...] = jnp.broadcast_to(jnp.where(excess, j + 1, big), js_ref.shape)

    return thr


def _dsa_prompt_kernel(iq_ref, misc_ref, q_ref, ik_ref, k_ref, v_ref, o_ref,
                       key_sc, js_sc, m_sc, l_sc, acc_sc, *, pad, topk, kc, idx_bits):
    i = pl.program_id(1)
    n_kc = (i * QB + QB - 1) // kc + 1
    row = i * QB + lax.broadcasted_iota(I32, (QB, 1), 0)

    misc = misc_ref[...]
    iq = iq_ref[...]
    iq4 = jnp.concatenate([iq[:, h * IDX_DIM:(h + 1) * IDX_DIM] for h in range(IDX_HEADS)], axis=0)

    ws = [misc[:, MISC_IW + h:MISC_IW + h + 1] for h in range(IDX_HEADS)]

    def admissible(col):
        return (col <= row) & (col >= pad)

    def p1(c, carry):
        off = pl.multiple_of(c * kc, kc)
        s4 = _dot_nt(iq4, ik_ref[pl.ds(off, kc), :])
        sc = jnp.zeros((QB, kc), F32)
        for h in range(IDX_HEADS):
            sc = sc + ws[h] * jnp.maximum(s4[h * QB:(h + 1) * QB, :], 0.0)
        col = off + lax.broadcasted_iota(I32, (QB, kc), 1)
        sc = jnp.where(admissible(col), sc + 0.0, -jnp.inf)
        key_sc[:, pl.ds(off, kc)] = _to_key(sc)
        return carry

    lax.fori_loop(0, n_kc, p1, 0)

    thr = _select_threshold(key_sc, js_sc, n_kc, kc, topk, idx_bits)
    js = js_sc[:, 0:1]

    m_sc[...] = jnp.full(m_sc.shape, NEG_BIG, F32)
    l_sc[...] = jnp.zeros_like(l_sc)
    acc_sc[...] = jnp.zeros_like(acc_sc)
    q = q_ref[...]
    lane = lax.broadcasted_iota(I32, (QB, LANE), 1)
    lo_half = lane < HEAD_DIM
    n_pair = ATT_HEADS // 2
    qm = []
    for p in range(n_pair):
        qp = q[:, p * LANE:(p + 1) * LANE]
        qm.append((jnp.where(lo_half, qp, 0).astype(BF16), jnp.where(lo_half, 0, qp).astype(BF16)))

    def p3(c, carry):
        off = pl.multiple_of(c * kc, kc)
        key = key_sc[:, pl.ds(off, kc)]
        col = off + lax.broadcasted_iota(I32, (QB, kc), 1)
        sel = ((key > thr) | ((key == thr) & (col < js))) & admissible(col)
        bias = jnp.where(sel, 0.0, NEG_BIG)
        for p in range(n_pair):
            kp = k_ref[pl.ds(off, kc), p * LANE:(p + 1) * LANE]
            vp = v_ref[pl.ds(off, kc), p * LANE:(p + 1) * LANE]
            pv, al = [], []
            for e in range(2):
                h = 2 * p + e
                s = _dot_nt(qm[p][e], kp) + bias
                m_old = m_sc[h]
                m_new = jnp.maximum(m_old, jnp.max(s, axis=1, keepdims=True))
                alpha = jnp.exp(m_old - m_new)
                pe = jnp.exp(s - m_new[:, 0:1])
                l_sc[h] = alpha * l_sc[h] + jnp.sum(pe, axis=1, keepdims=True)
                m_sc[h] = m_new
                pv.append(_dot(pe.astype(BF16), vp))
                al.append(alpha)
            sl = slice(p * LANE, (p + 1) * LANE)
            acc_sc[:, sl] = jnp.where(lo_half, al[0], al[1]) * acc_sc[:, sl] + jnp.where(lo_half, pv[0], pv[1])
        return carry

    lax.fori_loop(0, n_kc, p3, 0)

    valid = row >= pad
    for p in range(n_pair):
        sl = slice(p * LANE, (p + 1) * LANE)
        l = jnp.where(lo_half, l_sc[2 * p], l_sc[2 * p + 1])
        o_ref[:, sl] = jnp.where(valid, acc_sc[:, sl] / l, 0.0)


def _dsa_prompt(iq, misc, q, ikb, kb, vb, pad, topk):
    B, R, _ = q.shape
    kc = _tile(R, (384, 256, 128))
    assert kc >= topk
    idx_bits = max(1, (R - 1).bit_length())

    def qspec(width):
        return pl.BlockSpec((None, QB, width), lambda b, i: (b, i, 0))

    def fspec(width):
        return pl.BlockSpec((None, R, width), lambda b, i: (b, 0, 0))

    return pl.pallas_call(
        functools.partial(_dsa_prompt_kernel, pad=pad, topk=topk, kc=kc, idx_bits=idx_bits),
        grid=(B, R // QB),
        in_specs=[qspec(IDX_W), qspec(LANE), qspec(ATT_W), fspec(IDX_DIM), fspec(ATT_W), fspec(ATT_W)],
        out_specs=qspec(ATT_W),
        out_shape=jax.ShapeDtypeStruct((B, R, ATT_W), F32),
        scratch_shapes=[pltpu.VMEM((QB, R), I32), pltpu.VMEM((QB, LANE), I32),
                        pltpu.VMEM((ATT_HEADS, QB, LANE), F32), pltpu.VMEM((ATT_HEADS, QB, LANE), F32),
                        pltpu.VMEM((QB, ATT_W), F32)],
        compiler_params=_cparams(2),
        name="dsa_prompt",
    )(iq, misc, q, ikb, kb, vb)


QPAD = 8


def _idx_sample_kernel(pt_ref, iq_ref, w_ref, ikn_ref, *rest, pg, n_pg, topk, past, idx_bits, cw):
    pages = rest[:pg]
    key_ref, thr_ref, js_ref = rest[pg:pg + 3]
    g = pl.program_id(1)
    iq = iq_ref[...]
    w = w_ref[...]

    def score(ikp):
        r = jnp.maximum(_dot_nt(iq, ikp.astype(BF16)), 0.0) * w
        s = r[0:QPAD]
        for h in range(1, IDX_HEADS):
            s = s + r[h * QPAD:(h + 1) * QPAD]
        return s + 0.0

    for j in range(pg):
        off = pl.multiple_of((g * pg + j) * PAGE, PAGE)
        key_ref[:, pl.ds(off, PAGE)] = _to_key(score(pages[j][...]))

    @pl.when(g == n_pg - 1)
    def _():
        sn = score(ikn_ref[...])
        qrow = lax.broadcasted_iota(I32, (QPAD, PAGE), 0)
        colr = lax.broadcasted_iota(I32, (QPAD, PAGE), 1)
        key_ref[:, past:past + PAGE] = _to_key(jnp.where(colr <= qrow, sn, -jnp.inf))
        thr = _select_threshold(key_ref, js_ref, (past + PAGE) // cw, cw, topk, idx_bits)
        thr_ref[...] = jnp.broadcast_to(thr, thr_ref.shape)


def _idx_sample(page_table, iq, w, ikn, cache_ik, layer, topk):
    DB, n_pages = page_table.shape
    past = n_pages * PAGE
    pg = 16 if n_pages % 16 == 0 else (4 if n_pages % 4 == 0 else 1)
    n_pg = n_pages // pg
    ncol = past + PAGE
    cw = _tile(ncol, (2048, 1024, 512, 256, 128)) if ncol % 256 == 0 else PAGE
    idx_bits = max(1, (ncol - 1).bit_length())
    rows = IDX_HEADS * QPAD

    def page_spec(j):
        return pl.BlockSpec((None, None, PAGE, IDX_DIM), lambda b, g, pt: (layer, pt[b, g * pg + j], 0, 0))

    def bspec(*shape):
        nd = len(shape)
        return pl.BlockSpec((None,) + shape, lambda b, g, pt: (b,) + (0,) * nd)

    grid_spec = pltpu.PrefetchScalarGridSpec(
        num_scalar_prefetch=1,
        grid=(DB, n_pg),
        in_specs=[bspec(rows, IDX_DIM), bspec(rows, 1), bspec(PAGE, IDX_DIM)] + [page_spec(j) for j in range(pg)],
        out_specs=[bspec(QPAD, ncol), bspec(QPAD, LANE), bspec(QPAD, LANE)],
    )
    return pl.pallas_call(
        functools.partial(_idx_sample_kernel, pg=pg, n_pg=n_pg, topk=topk, past=past, idx_bits=idx_bits, cw=cw),
        grid_spec=grid_spec,
        out_shape=[jax.ShapeDtypeStruct((DB, QPAD, ncol), I32), jax.ShapeDtypeStruct((DB, QPAD, LANE), I32),
                   jax.ShapeDtypeStruct((DB, QPAD, LANE), I32)],
        compiler_params=_cparams(2),
        name="idx_sample",
    )(page_table, iq, w, ikn, *([cache_ik] * pg))


def _att_sample_kernel(pt_ref, q_ref, keyp_ref, keyn_ref, thr_ref, js_ref, kn_ref, vn_ref, *rest, pg, n_pg, past):
    kpages = rest[:pg]
    vpages = rest[pg:2 * pg]
    o_ref = rest[2 * pg]
    m_sc, l_sc, acc_sc = rest[2 * pg + 1:]
    g = pl.program_id(1)
    rows = ATT_HEADS * QPAD

    @pl.when(g == 0)
    def _():
        m_sc[...] = jnp.full(m_sc.shape, NEG_BIG, F32)
        l_sc[...] = jnp.zeros_like(l_sc)
        acc_sc[...] = jnp.zeros_like(acc_sc)

    q = q_ref[...]
    r_h = lax.broadcasted_iota(I32, (rows, ATT_W), 0) >> 3
    l_h = lax.broadcasted_iota(I32, (rows, ATT_W), 1) >> 6
    own = r_h == l_h
    qbd = jnp.where(own, jnp.concatenate([q.astype(F32)] * ATT_HEADS, axis=0), 0.0).astype(BF16)
    thr = thr_ref[...]
    js = js_ref[...]
    qpos = past + lax.broadcasted_iota(I32, (QPAD, PAGE), 0)

    def page(kp, vp, key, colbase):
        col = colbase + lax.broadcasted_iota(I32, (QPAD, PAGE), 1)
        sel = ((key > thr) | ((key == thr) & (col < js))) & (col <= qpos)
        bias8 = jnp.where(sel, 0.0, NEG_BIG)
        s = _dot_nt(qbd, kp.astype(BF16)) + jnp.concatenate([bias8] * ATT_HEADS, axis=0)
        m_old = m_sc[...]
        m_new = jnp.maximum(m_old, jnp.max(s, axis=1, keepdims=True))
        alpha = jnp.exp(m_old - m_new)
        pe = jnp.exp(s - m_new)
        l_sc[...] = alpha * l_sc[...] + jnp.sum(pe, axis=1, keepdims=True)
        m_sc[...] = m_new
        acc_sc[...] = alpha[:, 0:1] * acc_sc[...] + _dot(pe.astype(BF16), vp.astype(BF16))

    for j in range(pg):
        page(kpages[j][...], vpages[j][...], keyp_ref[:, j * PAGE:(j + 1) * PAGE], (g * pg + j) * PAGE)

    @pl.when(g == n_pg - 1)
    def _():
        page(kn_ref[...], vn_ref[...], keyn_ref[...], past)
        res = jnp.where(own, acc_sc[...] / l_sc[:, 0:1], 0.0)
        o = res[0:QPAD]
        for h in range(1, ATT_HEADS):
            o = o + res[h * QPAD:(h + 1) * QPAD]
        o_ref[...] = o


def _att_sample(page_table, q, keys, thr, js, kn, vn, cache_k, cache_v, layer):
    DB, n_pages = page_table.shape
    past = n_pages * PAGE
    pg = 4 if n_pages % 4 == 0 else 1
    n_pg = n_pages // pg
    rows = ATT_HEADS * QPAD

    def kv_spec(j):
        return pl.BlockSpec((None, None, PAGE, ATT_W), lambda b, g, pt: (layer, pt[b, g * pg + j], 0, 0))

    def bspec(*shape):
        nd = len(shape)
        return pl.BlockSpec((None,) + shape, lambda b, g, pt: (b,) + (0,) * nd)

    grid_spec = pltpu.PrefetchScalarGridSpec(
        num_scalar_prefetch=1,
        grid=(DB, n_pg),
        in_specs=[bspec(QPAD, ATT_W),
                  pl.BlockSpec((None, QPAD, pg * PAGE), lambda b, g, pt: (b, 0, g)),
                  pl.BlockSpec((None, QPAD, PAGE), lambda b, g, pt: (b, 0, n_pages)),
                  bspec(QPAD, LANE), bspec(QPAD, LANE), bspec(PAGE, ATT_W), bspec(PAGE, ATT_W)]
        + [kv_spec(j) for j in range(pg)] + [kv_spec(j) for j in range(pg)],
        out_specs=bspec(QPAD, ATT_W),
        scratch_shapes=[pltpu.VMEM((rows, PAGE), F32), pltpu.VMEM((rows, PAGE), F32), pltpu.VMEM((rows, ATT_W), F32)],
    )
    return pl.pallas_call(
        functools.partial(_att_sample_kernel, pg=pg, n_pg=n_pg, past=past),
        grid_spec=grid_spec,
        out_shape=jax.ShapeDtypeStruct((DB, QPAD, ATT_W), F32),
        compiler_params=_cparams(2),
        name="att_sample",
    )(page_table, q, keys, keys, thr, js, kn, vn, *([cache_k] * pg), *([cache_v] * pg))


def _conv3(u, w, first, second):
    u1 = jnp.where(first[0], first[1], pltpu.roll(u, 1, 0))
    u2 = jnp.where(second[0], second[1], pltpu.roll(u, 2, 0))
    return w[0:1] * u2 + w[1:2] * u1 + w[2:3] * u


def _post_kernel(*refs, sample, n_tok, n_ff):
    if sample:
        (h_ref, og_ref, scv_ref, oa_ref, p1s_ref, p2s_ref, p1f_ref, p2f_ref, wsc_ref, wout_ref, fnw_ref, wua_ref,
         wub_ref, wfc_ref, wdn_ref, h2_ref, u_ref, a_ref, acc_sc, hn_sc) = refs
    else:
        (h_ref, og_ref, scv_ref, oa_ref, wsc_ref, wout_ref, fnw_ref, wua_ref, wub_ref, wfc_ref, wdn_ref,
         h2_ref, sts_ref, stf_ref, cs_sc, cf_sc, acc_sc, hn_sc) = refs
    tm = h_ref.shape[0]

    if not sample:
        @pl.when(pl.program_id(1) == 0)
        def _():
            cs_sc[...] = jnp.zeros_like(cs_sc)
            cf_sc[...] = jnp.zeros_like(cf_sc)

    def fixes(width, c0, c1, p1, p2):
        rows = lax.broadcasted_iota(I32, (tm, width), 0)
        if sample:
            sp = lax.rem(rows, n_tok)
            return (sp < 1, p1), (sp < 2, p2)
        return (rows == 0, c1), (rows < 2, jnp.where(rows == 0, c0, c1))

    scv = scv_ref[...]
    sb = scv[:, 0:SCONV_W]
    u = scv[:, SCONV_W:2 * SCONV_W] * scv[:, 2 * SCONV_W:3 * SCONV_W]
    if sample:
        f1, f2 = fixes(SCONV_W, None, None, p1s_ref[...], p2s_ref[...])
        u_ref[...] = u
    else:
        f1, f2 = fixes(SCONV_W, cs_sc[0:1, :], cs_sc[1:2, :], None, None)
        cs_sc[0:2, :] = u[tm - 2:tm, :]
        sts_ref[...] = u[tm - 2:tm, :]
    o_s = sb * _conv3(u, wsc_ref[...], f1, f2)

    mix = (_dot(og_ref[...].astype(BF16), wout_ref[0:GLA_W, :])
           + _dot(o_s.astype(BF16), wout_ref[GLA_W:GLA_W + SCONV_W, :])
           + _dot(oa_ref[...].astype(BF16), wout_ref[GLA_W + SCONV_W:, :]))
    h1 = h_ref[...] + mix
    hn_sc[...] = (h1 * lax.rsqrt(jnp.mean(h1 * h1, axis=-1, keepdims=True) + EPS) * fnw_ref[...]).astype(BF16)
    acc_sc[...] = h1

    def ff(c, carry):
        hn = hn_sc[...]
        a = _dot(hn, wua_ref[c])
        bgate = _dot(hn, wub_ref[c])
        if sample:
            f1, f2 = fixes(FF_CHUNK, None, None, p1f_ref[c], p2f_ref[c])
            a_ref[c] = a
        else:
            f1, f2 = fixes(FF_CHUNK, cf_sc[c, 0:1, :], cf_sc[c, 1:2, :], None, None)
            cf_sc[c, 0:2, :] = a[tm - 2:tm, :]
            stf_ref[c] = a[tm - 2:tm, :]
        ac = _conv3(a, wfc_ref[c], f1, f2)
        act = (jax.nn.silu(ac) * bgate).astype(BF16)
        acc_sc[...] += _dot(act, wdn_ref[c])
        return carry

    lax.fori_loop(0, n_ff, ff, 0)
    h2_ref[...] = acc_sc[...]


def _post(h, og, scv, oa, wsc, wout, fnw, wua, wub, wfc, wdn, sample_fix=None, n_tok=1):
    B, R, D = h.shape
    n_ff = wua.shape[0]
    sample = sample_fix is not None
    tm = _tile(R)

    def rspec(width):
        return pl.BlockSpec((None, tm, width), lambda b, t: (b, t, 0))

    in_specs = [rspec(D), rspec(GLA_W), rspec(768), rspec(ATT_W)]
    args = [h, og, scv, oa]
    if sample:
        assert B == 1 and R == tm
        p1s, p2s, p1f, p2f = sample_fix
        in_specs += [_const_spec((tm, SCONV_W)), _const_spec((tm, SCONV_W)),
                     _const_spec((n_ff, tm, FF_CHUNK)), _const_spec((n_ff, tm, FF_CHUNK))]
        args += [p1s, p2s, p1f, p2f]
        out_specs = [rspec(D), rspec(SCONV_W), pl.BlockSpec((n_ff, tm, FF_CHUNK), lambda b, t: (0, 0, 0))]
        out_shape = [jax.ShapeDtypeStruct((B, R, D), F32), jax.ShapeDtypeStruct((B, R, SCONV_W), F32),
                     jax.ShapeDtypeStruct((n_ff, R, FF_CHUNK), F32)]
        scratch = []
    else:
        out_specs = [rspec(D), pl.BlockSpec((None, 2, SCONV_W), lambda b, t: (b, 0, 0)),
                     pl.BlockSpec((None, n_ff, 2, FF_CHUNK), lambda b, t: (b, 0, 0, 0))]
        out_shape = [jax.ShapeDtypeStruct((B, R, D), F32), jax.ShapeDtypeStruct((B, 2, SCONV_W), F32),
                     jax.ShapeDtypeStruct((B, n_ff, 2, FF_CHUNK), F32)]
        scratch = [pltpu.VMEM((8, SCONV_W), F32), pltpu.VMEM((n_ff, 8, FF_CHUNK), F32)]
    in_specs += [_const_spec((3, SCONV_W)), _const_spec((D, D)), _const_spec((1, D)),
                 _const_spec((n_ff, D, FF_CHUNK)), _const_spec((n_ff, D, FF_CHUNK)),
                 _const_spec((n_ff, 3, FF_CHUNK)), _const_spec((n_ff, FF_CHUNK, D))]
    args += [wsc, wout, fnw, wua, wub, wfc, wdn]
    scratch += [pltpu.VMEM((tm, D), F32), pltpu.VMEM((tm, D), BF16)]
    return pl.pallas_call(
        functools.partial(_post_kernel, sample=sample, n_tok=n_tok, n_ff=n_ff),
        grid=(B, R // tm),
        in_specs=in_specs,
        out_specs=out_specs,
        out_shape=out_shape,
        scratch_shapes=scratch,
        compiler_params=_cparams(2),
        name="post_sample" if sample else "post_prompt",
    )(*args)


def _norm_kernel(x_ref, w_ref, o_ref):
    x = x_ref[...]
    o_ref[...] = x * lax.rsqrt(jnp.mean(x * x, axis=-1, keepdims=True) + EPS) * w_ref[...]


def _final_norm(h, w, skip_rows, n_rows):
    B, _, D = h.shape
    tm = LANE if n_rows % LANE == 0 else n_rows
    assert skip_rows % tm == 0 and n_rows % tm == 0
    off = skip_rows // tm
    return pl.pallas_call(
        _norm_kernel,
        grid=(B, n_rows // tm),
        in_specs=[pl.BlockSpec((None, tm, D), lambda b, t: (b, t + off, 0)), _const_spec((1, D))],
        out_specs=pl.BlockSpec((None, tm, D), lambda b, t: (b, t, 0)),
        out_shape=jax.ShapeDtypeStruct((B, n_rows, D), F32),
        compiler_params=_cparams(2),
        name="final_norm",
    )(h, w)


def _rope_tables(pos):
    half = ROT_DIM // 2
    inv_freq = ROPE_THETA ** (-jnp.arange(half, dtype=F32) / half)
    ang = pos.astype(F32)[:, None] * inv_freq[None, :]
    cos, sin = jnp.cos(ang), jnp.sin(ang)
    n = pos.shape[0]
    one = jnp.ones((n, HEAD_DIM - ROT_DIM), F32)
    zero = jnp.zeros((n, HEAD_DIM - ROT_DIM), F32)
    z8 = jnp.zeros((n, half), F32)
    c64 = jnp.concatenate([cos, cos, one], axis=1)
    a64 = jnp.concatenate([-sin, z8, zero], axis=1)
    b64 = jnp.concatenate([z8, sin, zero], axis=1)
    return jnp.stack([jnp.tile(t, (1, LANE // HEAD_DIM)) for t in (c64, a64, b64)], axis=0)


def _layer_weights(l, attn_norm, w_in, w_gate_up, b_gate, gla_norm, w_sconv, w_out, ffn_norm, w_up, w_ffn_conv, w_down):
    D = w_in.shape[1]
    wi = w_in[l]
    cuts = [0, 128, 256, 512, 528, 784, 1040, 1296, 1552, 2064, 2576, 3088, 3344, 3408, 3412]
    gq, gk, gv, glr, gr, sb, sc, sx, aq, ak, av, iq, ik, iw = [wi[:, cuts[i]:cuts[i + 1]] for i in range(14)]
    pad = jnp.zeros((D, LANE - IDX_DIM - GLA_RANK - IDX_HEADS), F32)
    wp = jnp.concatenate([gq, gk, gv, gr, sb, sc, sx, aq, ak, av, iq, ik, glr, iw, pad], axis=1).astype(BF16)
    wg = jnp.zeros((LANE, GLA_QK), F32).at[MISC_GLR:MISC_GLR + GLA_RANK].set(w_gate_up[l])
    d_ff = w_down.shape[1]
    n_ff = d_ff // FF_CHUNK
    wu = w_up[l].astype(BF16)
    wua = wu[:, :d_ff].reshape(D, n_ff, FF_CHUNK).transpose(1, 0, 2)
    wub = wu[:, d_ff:].reshape(D, n_ff, FF_CHUNK).transpose(1, 0, 2)
    wfc = w_ffn_conv[l].reshape(3, n_ff, FF_CHUNK).transpose(1, 0, 2)
    wdn = w_down[l].astype(BF16).reshape(n_ff, FF_CHUNK, D)
    return dict(nw=attn_norm[l][None], wp=wp, wg=wg, bg=b_gate[l][None], gn=gla_norm[l][None], wsc=w_sconv[l],
                wout=w_out[l].astype(BF16), fnw=ffn_norm[l][None], wua=wua, wub=wub, wfc=wfc, wdn=wdn)


def _unchunk(a):
    a = jnp.swapaxes(a, -3, -2)
    return a.reshape(a.shape[:-2] + (a.shape[-2] * a.shape[-1],))


def kernel(x_prompt, x_sample, cache_k, cache_v, cache_idx_k, page_table, state_gla, state_sconv, state_ffn_conv,
           meta_tokens, attn_norm, w_in, w_gate_up, b_gate, gla_norm, w_sconv, w_out, ffn_norm, w_up, w_ffn_conv,
           w_down, final_norm):
    B, S_len, D = x_prompt.shape
    depth = w_in.shape[0]
    T = S_len + N_META
    pad = (-T) % LANE
    R = pad + T
    DB, S = x_sample.shape[:2]
    n_pages = page_table.shape[1]
    past = n_pages * PAGE
    topk_p = min(TOPK_MAX, T // 4)
    topk_s = min(TOPK_MAX, (past + S) // 4)
    d_ff = w_down.shape[1]
    n_ff = d_ff // FF_CHUNK
    n_pool = cache_k.shape[1]
    assert S <= QPAD and (DB * S) % 8 == 0

    hp = jnp.concatenate([jnp.zeros((B, pad, D), F32), jnp.broadcast_to(meta_tokens[None], (B, N_META, D)), x_prompt],
                         axis=1)
    hs = x_sample.reshape(1, DB * S, D)
    rope_p = _rope_tables(jnp.maximum(jnp.arange(R) - pad, 0))
    rope_s = _rope_tables(past + jnp.arange(DB * S) % S)
    ck = cache_k.reshape(depth, n_pool, PAGE, ATT_W)
    cv = cache_v.reshape(depth, n_pool, PAGE, ATT_W)

    outs_p = [[] for _ in range(6)]
    outs_s = [[] for _ in range(6)]
    for l in range(depth):
        W = _layer_weights(l, attn_norm, w_in, w_gate_up, b_gate, gla_norm, w_sconv, w_out, ffn_norm, w_up,
                           w_ffn_conv, w_down)
        post_w = (W["wsc"], W["wout"], W["fnw"], W["wua"], W["wub"], W["wfc"], W["wdn"])

        gla, g, scv, q, k, v, kb, vb, iq, misc, ikb = _proj(hp, W["nw"], W["wp"], W["wg"], W["bg"], rope_p)
        o_gla, st_t = _gla_prompt(gla, g, W["gn"])
        o_att = _dsa_prompt(iq, misc, q, ikb, kb, vb, pad, topk_p)
        hp, st_s, st_f = _post(hp, o_gla, scv, o_att, *post_w)
        outs_p[0].append(k[:, pad:].reshape(B, T, ATT_HEADS, HEAD_DIM))
        outs_p[1].append(v[:, pad:].reshape(B, T, ATT_HEADS, HEAD_DIM))
        outs_p[2].append(misc[:, pad:, :IDX_DIM])
        outs_p[3].append(jnp.swapaxes(st_t, -1, -2))
        outs_p[4].append(st_s)
        outs_p[5].append(_unchunk(st_f))

        gla, g, scv, q, k, v, _, _, iq, misc, _ = _proj(hs, W["nw"], W["wp"], W["wg"], W["bg"], rope_s)
        z = gla[0].reshape(DB, S, 768)
        tr = lambda a: jnp.swapaxes(a, 1, 2)
        eT = tr(g[0].reshape(DB, S, GLA_QK))
        qT = tr(z[:, :, 0:GLA_QK])
        kT = tr(z[:, :, GLA_QK:2 * GLA_QK])
        vx = jnp.repeat(z[:, :, 2 * GLA_QK:2 * GLA_QK + GLA_W].reshape(DB, S, GLA_HEADS, 1, GLA_DV), GLA_DK, axis=3)
        vx = vx.reshape(DB, S, GLA_QK, GLA_DV)
        gr = z[:, :, 2 * GLA_QK + GLA_W:]
        s0 = state_gla[l].reshape(DB, GLA_QK, GLA_DV)
        o_gla, s_new = _gla_sample(qT, kT, eT, vx, gr, W["gn"], s0)

        def hq(a, n_h, d):
            a = a.reshape(DB, S, n_h, d).transpose(0, 2, 1, 3)
            a = jnp.pad(a, ((0, 0), (0, 0), (0, QPAD - S), (0, 0)))
            return a.reshape(DB, n_h * QPAD, d)

        m0 = misc[0]
        iq_s = hq(iq[0], IDX_HEADS, IDX_DIM)
        w_s = hq(m0[:, MISC_IW:MISC_IW + IDX_HEADS], IDX_HEADS, 1)
        rowpad = lambda a: jnp.pad(a.reshape(DB, S, a.shape[-1]), ((0, 0), (0, PAGE - S), (0, 0)))
        keys, thr, js = _idx_sample(page_table, iq_s, w_s, rowpad(m0[:, :IDX_DIM]), cache_idx_k, l, topk_s)
        q_s = jnp.pad(q[0].reshape(DB, S, ATT_W), ((0, 0), (0, QPAD - S), (0, 0)))
        o_att = _att_sample(page_table, q_s, keys, thr, js, rowpad(k[0]), rowpad(v[0]), ck, cv, l)[:, :S]

        def fix(prev, shift):
            p = jnp.zeros((DB, S) + prev.shape[2:], F32)
            for s in range(shift):
                p = p.at[:, s].set(prev[:, 2 - shift + s])
            return p.reshape((DB * S,) + prev.shape[2:])

        def chunked(a):
            return a.reshape(DB * S, n_ff, FF_CHUNK).transpose(1, 0, 2)

        sfix = (fix(state_sconv[l], 1), fix(state_sconv[l], 2),
                chunked(fix(state_ffn_conv[l], 1)), chunked(fix(state_ffn_conv[l], 2)))
        hs, u, a = _post(hs, o_gla.reshape(1, DB * S, GLA_W), scv, o_att.reshape(1, DB * S, ATT_W), *post_w,
                         sample_fix=sfix, n_tok=S)
        outs_s[0].append(k[0].reshape(DB, S, ATT_HEADS, HEAD_DIM))
        outs_s[1].append(v[0].reshape(DB, S, ATT_HEADS, HEAD_DIM))
        outs_s[2].append(m0[:, :IDX_DIM].reshape(DB, S, IDX_DIM))
        outs_s[3].append(s_new.reshape(DB, GLA_HEADS, GLA_DK, GLA_DV))
        outs_s[4].append(u[0].reshape(DB, S, SCONV_W)[:, S - 2:])
        outs_s[5].append(_unchunk(a).reshape(DB, S, d_ff)[:, S - 2:])

    fw = final_norm[None]
    y_prompt = _final_norm(hp, fw, pad + N_META, S_len)
    y_sample = _final_norm(hs, fw, 0, DB * S).reshape(DB, S, D)
    sp = [jnp.stack(a, axis=0) for a in outs_p]
    ss = [jnp.stack(a, axis=0) for a in outs_s]
    return (y_prompt, y_sample, *sp, *ss)
```

```python
import functools

import jax
import jax.numpy as jnp
from jax import lax
from jax.experimental import pallas as pl
from jax.experimental.pallas import tpu as pltpu

F32 = jnp.float32
BF16 = jnp.bfloat16
I32 = jnp.int32

N_META = 16
GLA_HEADS, GLA_DK, GLA_DV = 4, 32, 64
GLA_RANK = 16
GLA_TAU = 16.0
GLA_W = GLA_HEADS * GLA_DV
GLA_QK = GLA_HEADS * GLA_DK
SCONV_W = 256
ATT_HEADS, HEAD_DIM = 8, 64
ATT_W = ATT_HEADS * HEAD_DIM
ROT_DIM = 16
ROPE_THETA = 500000.0
IDX_HEADS, IDX_DIM = 4, 64
IDX_W = IDX_HEADS * IDX_DIM
TOPK_MAX = 256
PAGE = 128
EPS = 1e-6
FF_CHUNK = 256

LANE = 128
QB = 128
GLA_CHUNK = 64
NEG_BIG = -1e30
VMEM_LIMIT = 56 * 1024 * 1024

SEG_GLA = (0, 768)
SEG_SCV = (768, 1536)
SEG_AQ = (1536, 2048)
SEG_IQ = (2048, 2304)
SEG_MISC = (2304, 2432)
N_PROJ = 2432
MISC_GLR = 0
MISC_IW = 16
TSEG_K = (0, 512)
TSEG_V = (512, 1024)
TSEG_IK = (1024, 1088)
N_PROJ_T = 1088

INT_MIN = -2147483648
NEG_INF_KEY = -2139095041


def _tile(n, cands=(512, 384, 256, 128, 64, 32, 16, 8)):
    for c in cands:
        if n % c == 0:
            return c
    raise ValueError(f"no tile for {n}")


def _dot(a, b):
    return jnp.dot(a, b, preferred_element_type=F32)


def _dot_nt(a, b):
    return lax.dot_general(a, b, (((1,), (1,)), ((), ())), preferred_element_type=F32)


def _dot_tn(a, b):
    return lax.dot_general(a, b, (((0,), (0,)), ((), ())), preferred_element_type=F32)


def _cparams(n_axes):
    return pltpu.CompilerParams(dimension_semantics=("arbitrary",) * n_axes, vmem_limit_bytes=VMEM_LIMIT)


def _const_spec(shape):
    nd = len(shape)
    return pl.BlockSpec(shape, lambda *_: (0,) * nd, pipeline_mode=pl.Buffered(1))


def _to_key(x):
    bits = pltpu.bitcast(x, I32)
    return bits ^ ((bits >> 31) & 0x7FFFFFFF)


def _proj_kernel(x_ref, nw_ref, w_ref, wt_ref, wg_ref, bg_ref, rope_ref, ropet_ref,
                 gla_ref, g_ref, scv_ref, q_ref, iq_ref, misc_ref,
                 kt_ref, vt_ref, ikt_ref, ktb_ref, vtb_ref, iktb_ref):
    x = x_ref[...]
    xn = x * lax.rsqrt(jnp.mean(x * x, axis=-1, keepdims=True) + EPS) * nw_ref[...]
    xb = xn.astype(BF16)

    def seg(s):
        return _dot(xb, w_ref[:, s[0]:s[1]])

    def seg_t(s):
        return _dot_nt(wt_ref[s[0]:s[1], :], xb)

    cos, s1, s2 = rope_ref[0], rope_ref[1], rope_ref[2]

    def rope128(z, c, a, b):
        return z * c + pltpu.roll(z, LANE - ROT_DIM // 2, 1) * a + pltpu.roll(z, ROT_DIM // 2, 1) * b

    gla_ref[...] = seg(SEG_GLA)
    scv_ref[...] = seg(SEG_SCV)

    zq = seg(SEG_AQ)
    for c in range(ATT_W // LANE):
        sl = slice(c * LANE, (c + 1) * LANE)
        q_ref[:, sl] = (rope128(zq[:, sl], cos, s1, s2) * HEAD_DIM ** -0.5).astype(BF16)
    zi = seg(SEG_IQ)
    for c in range(IDX_W // LANE):
        sl = slice(c * LANE, (c + 1) * LANE)
        iq_ref[:, sl] = (rope128(zi[:, sl], cos, s1, s2) * IDX_DIM ** -0.5).astype(BF16)

    zm = seg(SEG_MISC)
    gpre = jnp.dot(zm, wg_ref[...], preferred_element_type=F32, precision=lax.Precision.HIGHEST) + bg_ref[...]
    g_ref[...] = jax.nn.log_sigmoid(gpre) * (1.0 / GLA_TAU)
    lane = lax.broadcasted_iota(I32, zm.shape, 1)
    misc_ref[...] = zm * jnp.where((lane >= MISC_IW) & (lane < MISC_IW + IDX_HEADS), IDX_HEADS ** -0.5, 1.0)

    cos_t, sin_t = ropet_ref[0], ropet_ref[1]
    half = ROT_DIM // 2

    def store_roped_t(z, n_heads, ref32, ref16):
        for h in range(n_heads):
            r0 = h * HEAD_DIM
            x1, x2 = z[r0:r0 + half], z[r0 + half:r0 + 2 * half]
            parts = ((r0, x1 * cos_t - x2 * sin_t), (r0 + half, x2 * cos_t + x1 * sin_t),
                     (r0 + 2 * half, z[r0 + 2 * half:r0 + HEAD_DIM]))
            for start, val in parts:
                ref32[start:start + val.shape[0], :] = val
        ref16[...] = ref32[...].astype(BF16)

    store_roped_t(seg_t(TSEG_K), ATT_HEADS, kt_ref, ktb_ref)
    zv = seg_t(TSEG_V)
    vt_ref[...] = zv
    vtb_ref[...] = zv.astype(BF16)
    store_roped_t(seg_t(TSEG_IK), 1, ikt_ref, iktb_ref)


def _proj(x, nw, w, wt, wg, bg, rope, rope_t):
    B, R, D = x.shape
    tm = _tile(R)
    grid = (B, R // tm)

    def rspec(width):
        return pl.BlockSpec((None, tm, width), lambda b, t: (b, t, 0))

    def tspec(rows):
        return pl.BlockSpec((None, rows, tm), lambda b, t: (b, 0, t))

    outs = [(768, F32), (GLA_QK, F32), (768, F32), (ATT_W, BF16), (IDX_W, BF16), (LANE, F32)]
    outs_t = [(ATT_W, F32), (ATT_W, F32), (IDX_DIM, F32), (ATT_W, BF16), (ATT_W, BF16), (IDX_DIM, BF16)]
    return pl.pallas_call(
        _proj_kernel,
        grid=grid,
        in_specs=[rspec(D), _const_spec((1, D)), _const_spec((D, N_PROJ)), _const_spec((N_PROJ_T, D)),
                  _const_spec((LANE, GLA_QK)), _const_spec((1, GLA_QK)),
                  pl.BlockSpec((3, tm, LANE), lambda b, t: (0, t, 0)),
                  pl.BlockSpec((2, ROT_DIM // 2, tm), lambda b, t: (0, 0, t))],
        out_specs=[rspec(wd) for wd, _ in outs] + [tspec(r) for r, _ in outs_t],
        out_shape=[jax.ShapeDtypeStruct((B, R, wd), dt) for wd, dt in outs]
        + [jax.ShapeDtypeStruct((B, r, R), dt) for r, dt in outs_t],
        compiler_params=_cparams(2),
        name="proj",
    )(x, nw, w, wt, wg, bg, rope, rope_t)


def _gla_kernel(gla_ref, g_ref, gn_ref, o_ref, st_ref, s_sc):
    t = pl.program_id(1)
    C = GLA_CHUNK

    @pl.when(t == 0)
    def _():
        s_sc[...] = jnp.zeros_like(s_sc)

    z = gla_ref[...]
    g = g_ref[...]
    r_i = lax.broadcasted_iota(I32, (C, C), 0)
    c_i = lax.broadcasted_iota(I32, (C, C), 1)
    causal = c_i <= r_i
    tri = jnp.where(causal, 1.0, 0.0).astype(F32)
    b = jnp.dot(tri, g, preferred_element_type=F32, precision=lax.Precision.HIGHEST)
    b_last = b[C - 1:C, :]
    b_mid = b[C // 2:C // 2 + 1, :]
    e_in = jnp.exp(b)
    e_q = jnp.exp(b - b_mid)
    e_k = jnp.exp(b_mid - b)
    e_out = jnp.exp(b_last - b)
    e_last = jnp.exp(b_last)
    qz = z[:, 0:GLA_QK] * GLA_DK ** -0.5
    kz = z[:, GLA_QK:2 * GLA_QK]
    q_in = (qz * e_in).astype(BF16)
    q_x = (qz * e_q).astype(BF16)
    k_x = (kz * e_k).astype(BF16)
    k_out = (kz * e_out).astype(BF16)
    gn = gn_ref[...]
    for h in range(GLA_HEADS):
        ks = slice(h * GLA_DK, (h + 1) * GLA_DK)
        v = z[:, 2 * GLA_QK + h * GLA_DV:2 * GLA_QK + (h + 1) * GLA_DV]
        gr = z[:, 2 * GLA_QK + GLA_W + h * GLA_DV:2 * GLA_QK + GLA_W + (h + 1) * GLA_DV]
        vb = v.astype(BF16)
        st = s_sc[h]
        att = jnp.where(causal, _dot_nt(q_x[:, ks], k_x[:, ks]), 0.0)
        o = _dot(att.astype(BF16), vb) + _dot_nt(q_in[:, ks], st.astype(BF16))
        s_sc[h] = st * e_last[:, ks] + _dot_tn(vb, k_out[:, ks])
        on = o * lax.rsqrt(jnp.mean(o * o, axis=-1, keepdims=True) + EPS) * gn
        o_ref[:, h * GLA_DV:(h + 1) * GLA_DV] = on * jax.nn.silu(gr)
    st_ref[...] = s_sc[...]


def _gla_prompt(gla, g, gn):
    B, R, _ = gla.shape
    C = GLA_CHUNK
    return pl.pallas_call(
        _gla_kernel,
        grid=(B, R // C),
        in_specs=[pl.BlockSpec((None, C, 768), lambda b, t: (b, t, 0)),
                  pl.BlockSpec((None, C, GLA_QK), lambda b, t: (b, t, 0)),
                  _const_spec((1, GLA_DV))],
        out_specs=[pl.BlockSpec((None, C, GLA_W), lambda b, t: (b, t, 0)),
                   pl.BlockSpec((None, GLA_HEADS, GLA_DV, GLA_DK), lambda b, t: (b, 0, 0, 0))],
        out_shape=[jax.ShapeDtypeStruct((B, R, GLA_W), F32),
                   jax.ShapeDtypeStruct((B, GLA_HEADS, GLA_DV, GLA_DK), F32)],
        scratch_shapes=[pltpu.VMEM((GLA_HEADS, GLA_DV, GLA_DK), F32)],
        compiler_params=_cparams(2),
        name="gla_prompt",
    )(gla, g, gn)


def _gla_step_kernel(qT_ref, kT_ref, gT_ref, vx_ref, gr_ref, gn_ref, s0_ref, o_ref, s_ref, *, n_tok):
    S = s0_ref[...]
    qT, kT, eT = qT_ref[...] * GLA_DK ** -0.5, kT_ref[...], jnp.exp(gT_ref[...])
    gn = gn_ref[...]
    gr = gr_ref[...]
    for t in range(n_tok):
        S = eT[:, t:t + 1] * S + kT[:, t:t + 1] * vx_ref[t]
        qs = qT[:, t:t + 1] * S
        for h in range(GLA_HEADS):
            o = jnp.sum(qs[h * GLA_DK:(h + 1) * GLA_DK, :], axis=0, keepdims=True)
            on = o * lax.rsqrt(jnp.mean(o * o, axis=-1, keepdims=True) + EPS) * gn
            hs = slice(h * GLA_DV, (h + 1) * GLA_DV)
            o_ref[t:t + 1, hs] = on * jax.nn.silu(gr[t:t + 1, hs])
    s_ref[...] = S


def _gla_sample(qT, kT, eT, vx, gr, gn, s0):
    DB, HK, n_tok = qT.shape

    def bspec(*shape):
        nd = len(shape)
        return pl.BlockSpec((None,) + shape, lambda b: (b,) + (0,) * nd)

    return pl.pallas_call(
        functools.partial(_gla_step_kernel, n_tok=n_tok),
        grid=(DB,),
        in_specs=[bspec(HK, n_tok), bspec(HK, n_tok), bspec(HK, n_tok), bspec(n_tok, HK, GLA_DV),
                  bspec(n_tok, GLA_W), _const_spec((1, GLA_DV)), bspec(HK, GLA_DV)],
        out_specs=[bspec(n_tok, GLA_W), bspec(HK, GLA_DV)],
        out_shape=[jax.ShapeDtypeStruct((DB, n_tok, GLA_W), F32), jax.ShapeDtypeStruct((DB, HK, GLA_DV), F32)],
        compiler_params=_cparams(1),
        name="gla_sample",
    )(qT, kT, eT, vx, gr, gn, s0)


def _count_cols(key_ref, rsl, rows, n_chunks, cw, pred):
    def body(c, acc):
        off = pl.multiple_of(c * cw, cw)
        col = off + lax.broadcasted_iota(I32, (rows, LANE), 1)
        for j in range(cw // LANE):
            k = key_ref[rsl, pl.ds(off + j * LANE, LANE)]
            acc = acc + jnp.where(pred(k, col + j * LANE), 1, 0)
        return acc

    zero = jnp.zeros((rows, LANE), I32)
    return body(0, zero) if isinstance(n_chunks, int) and n_chunks == 1 else lax.fori_loop(0, n_chunks, body, zero)


def _selected(key, col, thr, js):
    return (key - jnp.where(col >= js, 1, 0)) >= thr


def _select_threshold(key_ref, thr_ref, js_ref, n_groups, rows, n_chunks, cw, topk, idx_bits):
    gs = range(n_groups)
    rsl = [slice(g * rows, (g + 1) * rows) for g in gs]

    def partial_count(g, pred):
        return _count_cols(key_ref, rsl[g], rows, n_chunks, cw, pred)

    def count(g, pred):
        return jnp.sum(partial_count(g, pred), axis=1, keepdims=True)

    def start(g):
        c0 = count(g, lambda k, col: k >= 0)
        return jnp.where(c0 >= topk, 0, INT_MIN).astype(I32)

    def bs(it, los):
        bit = lax.shift_left(jnp.int32(1), 30 - it)
        cands = [los[g] + bit for g in gs]
        parts = [partial_count(g, lambda k, col, cand=cands[g]: k >= cand) for g in gs]
        return tuple(jnp.where(jnp.sum(parts[g], axis=1, keepdims=True) >= topk, cands[g], los[g]) for g in gs)

    thrs = lax.fori_loop(0, 31, bs, tuple(start(g) for g in gs))
    for g in gs:
        thr = thrs[g]
        n_gt = count(g, lambda k, col: k > thr)
        n_ge = count(g, lambda k, col: k >= thr)
        need = topk - n_gt
        excess = (n_ge > topk) & (thr > NEG_INF_KEY)
        base = jnp.where(thr > NEG_INF_KEY, 1 << idx_bits, 0).astype(I32)
        thr_ref[rsl[g], :] = jnp.broadcast_to(thr, (rows, LANE))
        js_ref[rsl[g], :] = jnp.broadcast_to(base, (rows, LANE))

        @pl.when(jnp.max(jnp.where(excess, 1, 0)) > 0)
        def _(g=g, thr=thr, need=need, excess=excess, base=base):
            def bs2(it, j):
                cand = j + lax.shift_left(jnp.int32(1), idx_bits - 1 - it)
                c = count(g, lambda k, col: (k == thr) & (col < cand))
                return jnp.where(c < need, cand, j)

            j = lax.fori_loop(0, idx_bits, bs2, jnp.zeros((rows, 1), I32))
            js_ref[rsl[g], :] = jnp.broadcast_to(jnp.where(excess, j + 1, base), (rows, LANE))


def _dsa_prompt_kernel(iq_ref, misc_ref, q_ref, ikt_ref, kt_ref, vt_ref, o_ref,
                       key_sc, thr_sc, js_sc, mp_sc, lp_sc, acc_sc, *, pad, topk, qb, idx_bits):
    i = pl.program_id(1)
    kc = qb
    n_kc = i + 1
    n_slab = kc // LANE
    row = i * qb + lax.broadcasted_iota(I32, (qb, 1), 0)

    def cols(c):
        return pl.ds(pl.multiple_of(c * kc, kc), kc)

    misc = misc_ref[...]
    iq = iq_ref[...]
    iqh = [iq[:, h * IDX_DIM:(h + 1) * IDX_DIM] for h in range(IDX_HEADS)]
    ws = [misc[:, MISC_IW + h:MISC_IW + h + 1] for h in range(IDX_HEADS)]

    def p1(c, carry):
        ikt = ikt_ref[:, cols(c)]
        sc = ws[0] * jnp.maximum(_dot(iqh[0], ikt), 0.0)
        for h in range(1, IDX_HEADS):
            sc = sc + ws[h] * jnp.maximum(_dot(iqh[h], ikt), 0.0)
        col = c * kc + lax.broadcasted_iota(I32, (qb, kc), 1)
        sc = jnp.where((col <= row) & (col >= pad), sc + 0.0, -jnp.inf)
        key_sc[:, cols(c)] = _to_key(sc)
        return carry

    lax.fori_loop(0, n_kc, p1, 0)

    n_groups = 4
    _select_threshold(key_sc, thr_sc, js_sc, n_groups, qb // n_groups, n_kc, kc, topk, idx_bits)
    thr = thr_sc[:, 0:1]
    js = js_sc[:, 0:1]

    def pmask(c, carry):
        col = c * kc + lax.broadcasted_iota(I32, (qb, kc), 1)
        sel = _selected(key_sc[:, cols(c)], col, thr, js)
        key_sc[:, cols(c)] = pltpu.bitcast(jnp.where(sel, 0.0, NEG_BIG), I32)
        return carry

    lax.fori_loop(0, n_kc, pmask, 0)

    q = q_ref[...]
    qh = [q[:, h * HEAD_DIM:(h + 1) * HEAD_DIM] for h in range(ATT_HEADS)]

    def scores(c, h):
        hs = slice(h * HEAD_DIM, (h + 1) * HEAD_DIM)
        return _dot(qh[h], kt_ref[hs, cols(c)]) + pltpu.bitcast(key_sc[:, cols(c)], F32)

    mp_sc[...] = jnp.full(mp_sc.shape, NEG_BIG, F32)

    def pa(c, carry):
        for h in range(ATT_HEADS):
            s = scores(c, h)
            mp = mp_sc[h]
            for j in range(n_slab):
                mp = jnp.maximum(mp, s[:, j * LANE:(j + 1) * LANE])
            mp_sc[h] = mp
        return carry

    lax.fori_loop(0, n_kc, pa, 0)
    for h in range(ATT_HEADS):
        mp_sc[h] = jnp.broadcast_to(jnp.max(mp_sc[h], axis=1, keepdims=True), (qb, LANE))

    lp_sc[...] = jnp.zeros_like(lp_sc)
    acc_sc[...] = jnp.zeros_like(acc_sc)

    def pb(c, carry):
        for h in range(ATT_HEADS):
            hs = slice(h * HEAD_DIM, (h + 1) * HEAD_DIM)
            s = scores(c, h)
            m = mp_sc[h]
            ps = [jnp.exp(s[:, j * LANE:(j + 1) * LANE] - m) for j in range(n_slab)]
            lp = lp_sc[h]
            for pj in ps:
                lp = lp + pj
            lp_sc[h] = lp
            p = jnp.concatenate(ps, axis=1).astype(BF16)
            acc_sc[h] += _dot_nt(p, vt_ref[hs, cols(c)])
        return carry

    lax.fori_loop(0, n_kc, pb, 0)

    valid = row >= pad
    for h in range(ATT_HEADS):
        l = jnp.sum(lp_sc[h], axis=1, keepdims=True)
        o_ref[:, h * HEAD_DIM:(h + 1) * HEAD_DIM] = jnp.where(valid, acc_sc[h] / l, 0.0)


def _dsa_prompt(iq, misc, q, iktb, ktb, vtb, pad, topk):
    B, R, _ = q.shape
    qb = _tile(R, (384, 256, 128))
    assert qb >= topk
    idx_bits = max(1, (R - 1).bit_length())

    def qspec(width):
        return pl.BlockSpec((None, qb, width), lambda b, i: (b, i, 0))

    def fspec(rows):
        return pl.BlockSpec((None, rows, R), lambda b, i: (b, 0, 0))

    return pl.pallas_call(
        functools.partial(_dsa_prompt_kernel, pad=pad, topk=topk, qb=qb, idx_bits=idx_bits),
        grid=(B, R // qb),
        in_specs=[qspec(IDX_W), qspec(LANE), qspec(ATT_W), fspec(IDX_DIM), fspec(ATT_W), fspec(ATT_W)],
        out_specs=qspec(ATT_W),
        out_shape=jax.ShapeDtypeStruct((B, R, ATT_W), F32),
        scratch_shapes=[pltpu.VMEM((qb, R), I32), pltpu.VMEM((qb, LANE), I32), pltpu.VMEM((qb, LANE), I32),
                        pltpu.VMEM((ATT_HEADS, qb, LANE), F32), pltpu.VMEM((ATT_HEADS, qb, LANE), F32),
                        pltpu.VMEM((ATT_HEADS, qb, HEAD_DIM), F32)],
        compiler_params=_cparams(2),
        name="dsa_prompt",
    )(iq, misc, q, iktb, ktb, vtb)


QPAD = 8


def _idx_sample_kernel(pt_ref, iq_ref, w_ref, iknt_ref, *rest, pg, n_pg, topk, past, idx_bits):
    pages = rest[:pg]
    key_ref, thr_ref, js_ref = rest[pg:pg + 3]
    g = pl.program_id(1)
    iq = iq_ref[...]
    w = w_ref[...]

    def score(ikt):
        r = jnp.maximum(_dot(iq, ikt), 0.0) * w
        s = r[0:QPAD]
        for h in range(1, IDX_HEADS):
            s = s + r[h * QPAD:(h + 1) * QPAD]
        return s + 0.0

    ikt = jnp.concatenate([p[...].astype(BF16) for p in pages], axis=1)
    off = pl.multiple_of(g * (pg * PAGE), pg * PAGE)
    key_ref[:, pl.ds(off, pg * PAGE)] = _to_key(score(ikt))

    @pl.when(g == n_pg - 1)
    def _():
        sn = score(iknt_ref[...].astype(BF16))
        qrow = lax.broadcasted_iota(I32, (QPAD, PAGE), 0)
        colr = lax.broadcasted_iota(I32, (QPAD, PAGE), 1)
        key_ref[:, past:past + PAGE] = _to_key(jnp.where(colr <= qrow, sn, -jnp.inf))
        _select_threshold(key_ref, thr_ref, js_ref, 1, QPAD, 1, past + PAGE, topk, idx_bits)


def _idx_sample(page_table, iq, w, iknt, cache_ikt, layer, topk):
    DB, n_pages = page_table.shape
    past = n_pages * PAGE
    pg = 16 if n_pages % 16 == 0 else (4 if n_pages % 4 == 0 else 1)
    n_pg = n_pages // pg
    ncol = past + PAGE
    idx_bits = max(1, (ncol - 1).bit_length())
    rows = IDX_HEADS * QPAD

    def page_spec(j):
        return pl.BlockSpec((None, None, IDX_DIM, PAGE), lambda b, g, pt: (layer, pt[b, g * pg + j], 0, 0))

    def bspec(*shape):
        nd = len(shape)
        return pl.BlockSpec((None,) + shape, lambda b, g, pt: (b,) + (0,) * nd)

    grid_spec = pltpu.PrefetchScalarGridSpec(
        num_scalar_prefetch=1,
        grid=(DB, n_pg),
        in_specs=[bspec(rows, IDX_DIM), bspec(rows, 1), bspec(IDX_DIM, PAGE)] + [page_spec(j) for j in range(pg)],
        out_specs=[bspec(QPAD, ncol), bspec(QPAD, LANE), bspec(QPAD, LANE)],
    )
    return pl.pallas_call(
        functools.partial(_idx_sample_kernel, pg=pg, n_pg=n_pg, topk=topk, past=past, idx_bits=idx_bits),
        grid_spec=grid_spec,
        out_shape=[jax.ShapeDtypeStruct((DB, QPAD, ncol), I32), jax.ShapeDtypeStruct((DB, QPAD, LANE), I32),
                   jax.ShapeDtypeStruct((DB, QPAD, LANE), I32)],
        compiler_params=_cparams(2),
        name="idx_sample",
    )(page_table, iq, w, iknt, *([cache_ikt] * pg))


def _att_sample_kernel(pt_ref, q_ref, keyp_ref, keyn_ref, thr_ref, js_ref, kn_ref, vn_ref, *rest, pg, n_pg, past):
    kpages = rest[:pg]
    vpages = rest[pg:2 * pg]
    o_ref = rest[2 * pg]
    m_sc, l_sc, acc_sc = rest[2 * pg + 1:]
    g = pl.program_id(1)
    rows = ATT_HEADS * QPAD

    @pl.when(g == 0)
    def _():
        m_sc[...] = jnp.full(m_sc.shape, NEG_BIG, F32)
        l_sc[...] = jnp.zeros_like(l_sc)
        acc_sc[...] = jnp.zeros_like(acc_sc)

    q = q_ref[...]
    r_h = lax.broadcasted_iota(I32, (rows, ATT_W), 0) >> 3
    l_h = lax.broadcasted_iota(I32, (rows, ATT_W), 1) >> 6
    own = r_h == l_h
    qbd = jnp.where(own, jnp.concatenate([q.astype(F32)] * ATT_HEADS, axis=0), 0.0).astype(BF16)
    thr = thr_ref[:, 0:1]
    js = js_ref[:, 0:1]

    def update(kts, vts, key, colbase):
        width = key.shape[1]
        col = colbase + lax.broadcasted_iota(I32, (QPAD, width), 1)
        bias8 = jnp.where(_selected(key, col, thr, js), 0.0, NEG_BIG)
        s = jnp.concatenate([_dot(qbd, kt.astype(BF16)) for kt in kts], axis=1)
        s = s + jnp.concatenate([bias8] * ATT_HEADS, axis=0)
        m_old = m_sc[...]
        m_new = jnp.maximum(m_old, jnp.max(s, axis=1, keepdims=True))
        alpha = jnp.exp(m_old - m_new)
        pe = jnp.exp(s - m_new[:, 0:1])
        l_sc[...] = alpha * l_sc[...] + jnp.sum(pe, axis=1, keepdims=True)
        m_sc[...] = m_new
        pv = _dot_nt(pe[:, 0:PAGE].astype(BF16), vts[0].astype(BF16))
        for j in range(1, len(vts)):
            pv = pv + _dot_nt(pe[:, j * PAGE:(j + 1) * PAGE].astype(BF16), vts[j].astype(BF16))
        acc_sc[...] = alpha[:, 0:1] * acc_sc[...] + pv

    update([p[...] for p in kpages], [p[...] for p in vpages], keyp_ref[...], g * (pg * PAGE))

    @pl.when(g == n_pg - 1)
    def _():
        update([kn_ref[...]], [vn_ref[...]], keyn_ref[...], past)
        res = jnp.where(own, acc_sc[...] / l_sc[:, 0:1], 0.0)
        o = res[0:QPAD]
        for h in range(1, ATT_HEADS):
            o = o + res[h * QPAD:(h + 1) * QPAD]
        o_ref[...] = o


def _att_sample(page_table, q, keys, thr, js, kn, vn, cache_k, cache_v, layer):
    DB, n_pages = page_table.shape
    past = n_pages * PAGE
    pg = 8 if n_pages % 8 == 0 else (4 if n_pages % 4 == 0 else 1)
    n_pg = n_pages // pg
    rows = ATT_HEADS * QPAD

    def kv_spec(j):
        return pl.BlockSpec((None, None, ATT_W, PAGE), lambda b, g, pt: (layer, pt[b, g * pg + j], 0, 0))

    def bspec(*shape):
        nd = len(shape)
        return pl.BlockSpec((None,) + shape, lambda b, g, pt: (b,) + (0,) * nd)

    grid_spec = pltpu.PrefetchScalarGridSpec(
        num_scalar_prefetch=1,
        grid=(DB, n_pg),
        in_specs=[bspec(QPAD, ATT_W),
                  pl.BlockSpec((None, QPAD, pg * PAGE), lambda b, g, pt: (b, 0, g)),
                  pl.BlockSpec((None, QPAD, PAGE), lambda b, g, pt: (b, 0, n_pages)),
                  bspec(QPAD, LANE), bspec(QPAD, LANE), bspec(ATT_W, PAGE), bspec(ATT_W, PAGE)]
        + [kv_spec(j) for j in range(pg)] + [kv_spec(j) for j in range(pg)],
        out_specs=bspec(QPAD, ATT_W),
        scratch_shapes=[pltpu.VMEM((rows, PAGE), F32), pltpu.VMEM((rows, PAGE), F32), pltpu.VMEM((rows, ATT_W), F32)],
    )
    return pl.pallas_call(
        functools.partial(_att_sample_kernel, pg=pg, n_pg=n_pg, past=past),
        grid_spec=grid_spec,
        out_shape=jax.ShapeDtypeStruct((DB, QPAD, ATT_W), F32),
        compiler_params=_cparams(2),
        name="att_sample",
    )(page_table, q, keys, keys, thr, js, kn, vn, *([cache_k] * pg), *([cache_v] * pg))


def _conv3(u, w, first, second):
    u1 = jnp.where(first[0], first[1], pltpu.roll(u, 1, 0))
    u2 = jnp.where(second[0], second[1], pltpu.roll(u, 2, 0))
    return w[0:1] * u2 + w[1:2] * u1 + w[2:3] * u


def _post_kernel(*refs, sample, n_tok, n_ff):
    if sample:
        (h_ref, og_ref, scv_ref, oa_ref, p1s_ref, p2s_ref, p1f_ref, p2f_ref, wsc_ref, wout_ref, fnw_ref, wua_ref,
         wub_ref, wfc_ref, wdn_ref, h2_ref, u_ref, a_ref, acc_sc, hn_sc) = refs
    else:
        (h_ref, og_ref, scv_ref, oa_ref, wsc_ref, wout_ref, fnw_ref, wua_ref, wub_ref, wfc_ref, wdn_ref,
         h2_ref, sts_ref, stf_ref, cs_sc, cf_sc, acc_sc, hn_sc) = refs
    tm = h_ref.shape[0]

    if not sample:
        @pl.when(pl.program_id(1) == 0)
        def _():
            cs_sc[...] = jnp.zeros_like(cs_sc)
            cf_sc[...] = jnp.zeros_like(cf_sc)

    def fixes(width, c0, c1, p1, p2):
        rows = lax.broadcasted_iota(I32, (tm, width), 0)
        if sample:
            sp = lax.rem(rows, n_tok)
            return (sp < 1, p1), (sp < 2, p2)
        return (rows == 0, c1), (rows < 2, jnp.where(rows == 0, c0, c1))

    scv = scv_ref[...]
    sb = scv[:, 0:SCONV_W]
    u = scv[:, SCONV_W:2 * SCONV_W] * scv[:, 2 * SCONV_W:3 * SCONV_W]
    if sample:
        f1, f2 = fixes(SCONV_W, None, None, p1s_ref[...], p2s_ref[...])
        u_ref[...] = u
    else:
        f1, f2 = fixes(SCONV_W, cs_sc[0:1, :], cs_sc[1:2, :], None, None)
        cs_sc[0:2, :] = u[tm - 2:tm, :]
        sts_ref[...] = u[tm - 2:tm, :]
    o_s = sb * _conv3(u, wsc_ref[...], f1, f2)

    mix = (_dot(og_ref[...].astype(BF16), wout_ref[0:GLA_W, :])
           + _dot(o_s.astype(BF16), wout_ref[GLA_W:GLA_W + SCONV_W, :])
           + _dot(oa_ref[...].astype(BF16), wout_ref[GLA_W + SCONV_W:, :]))
    h1 = h_ref[...] + mix
    hn_sc[...] = (h1 * lax.rsqrt(jnp.mean(h1 * h1, axis=-1, keepdims=True) + EPS) * fnw_ref[...]).astype(BF16)
    acc_sc[...] = h1

    def ff(c, carry):
        hn = hn_sc[...]
        a = _dot(hn, wua_ref[c])
        bgate = _dot(hn, wub_ref[c])
        if sample:
            f1, f2 = fixes(FF_CHUNK, None, None, p1f_ref[c], p2f_ref[c])
            a_ref[c] = a
        else:
            f1, f2 = fixes(FF_CHUNK, cf_sc[c, 0:1, :], cf_sc[c, 1:2, :], None, None)
            cf_sc[c, 0:2, :] = a[tm - 2:tm, :]
            stf_ref[c] = a[tm - 2:tm, :]
        ac = _conv3(a, wfc_ref[c], f1, f2)
        act = (jax.nn.silu(ac) * bgate).astype(BF16)
        acc_sc[...] += _dot(act, wdn_ref[c])
        return carry

    lax.fori_loop(0, n_ff, ff, 0)
    h2_ref[...] = acc_sc[...]


def _post(h, og, scv, oa, wsc, wout, fnw, wua, wub, wfc, wdn, sample_fix=None, n_tok=1):
    B, R, D = h.shape
    n_ff = wua.shape[0]
    sample = sample_fix is not None
    tm = _tile(R)

    def rspec(width):
        return pl.BlockSpec((None, tm, width), lambda b, t: (b, t, 0))

    in_specs = [rspec(D), rspec(GLA_W), rspec(768), rspec(ATT_W)]
    args = [h, og, scv, oa]
    if sample:
        assert B == 1 and R == tm
        p1s, p2s, p1f, p2f = sample_fix
        in_specs += [_const_spec((tm, SCONV_W)), _const_spec((tm, SCONV_W)),
                     _const_spec((n_ff, tm, FF_CHUNK)), _const_spec((n_ff, tm, FF_CHUNK))]
        args += [p1s, p2s, p1f, p2f]
        out_specs = [rspec(D), rspec(SCONV_W), pl.BlockSpec((n_ff, tm, FF_CHUNK), lambda b, t: (0, 0, 0))]
        out_shape = [jax.ShapeDtypeStruct((B, R, D), F32), jax.ShapeDtypeStruct((B, R, SCONV_W), F32),
                     jax.ShapeDtypeStruct((n_ff, R, FF_CHUNK), F32)]
        scratch = []
    else:
        out_specs = [rspec(D), pl.BlockSpec((None, 2, SCONV_W), lambda b, t: (b, 0, 0)),
                     pl.BlockSpec((None, n_ff, 2, FF_CHUNK), lambda b, t: (b, 0, 0, 0))]
        out_shape = [jax.ShapeDtypeStruct((B, R, D), F32), jax.ShapeDtypeStruct((B, 2, SCONV_W), F32),
                     jax.ShapeDtypeStruct((B, n_ff, 2, FF_CHUNK), F32)]
        scratch = [pltpu.VMEM((8, SCONV_W), F32), pltpu.VMEM((n_ff, 8, FF_CHUNK), F32)]
    in_specs += [_const_spec((3, SCONV_W)), _const_spec((D, D)), _const_spec((1, D)),
                 _const_spec((n_ff, D, FF_CHUNK)), _const_spec((n_ff, D, FF_CHUNK)),
                 _const_spec((n_ff, 3, FF_CHUNK)), _const_spec((n_ff, FF_CHUNK, D))]
    args += [wsc, wout, fnw, wua, wub, wfc, wdn]
    scratch += [pltpu.VMEM((tm, D), F32), pltpu.VMEM((tm, D), BF16)]
    return pl.pallas_call(
        functools.partial(_post_kernel, sample=sample, n_tok=n_tok, n_ff=n_ff),
        grid=(B, R // tm),
        in_specs=in_specs,
        out_specs=out_specs,
        out_shape=out_shape,
        scratch_shapes=scratch,
        compiler_params=_cparams(2),
        name="post_sample" if sample else "post_prompt",
    )(*args)


def _norm_kernel(x_ref, w_ref, o_ref):
    x = x_ref[...]
    o_ref[...] = x * lax.rsqrt(jnp.mean(x * x, axis=-1, keepdims=True) + EPS) * w_ref[...]


def _final_norm(h, w, skip_rows, n_rows):
    B, _, D = h.shape
    tm = LANE if n_rows % LANE == 0 else n_rows
    assert skip_rows % tm == 0 and n_rows % tm == 0
    off = skip_rows // tm
    return pl.pallas_call(
        _norm_kernel,
        grid=(B, n_rows // tm),
        in_specs=[pl.BlockSpec((None, tm, D), lambda b, t: (b, t + off, 0)), _const_spec((1, D))],
        out_specs=pl.BlockSpec((None, tm, D), lambda b, t: (b, t, 0)),
        out_shape=jax.ShapeDtypeStruct((B, n_rows, D), F32),
        compiler_params=_cparams(2),
        name="final_norm",
    )(h, w)


def _rope_tables(pos):
    half = ROT_DIM // 2
    inv_freq = ROPE_THETA ** (-jnp.arange(half, dtype=F32) / half)
    ang = pos.astype(F32)[:, None] * inv_freq[None, :]
    cos, sin = jnp.cos(ang), jnp.sin(ang)
    n = pos.shape[0]
    one = jnp.ones((n, HEAD_DIM - ROT_DIM), F32)
    zero = jnp.zeros((n, HEAD_DIM - ROT_DIM), F32)
    z8 = jnp.zeros((n, half), F32)
    c64 = jnp.concatenate([cos, cos, one], axis=1)
    a64 = jnp.concatenate([-sin, z8, zero], axis=1)
    b64 = jnp.concatenate([z8, sin, zero], axis=1)
    row_major = jnp.stack([jnp.tile(t, (1, LANE // HEAD_DIM)) for t in (c64, a64, b64)], axis=0)
    return row_major, jnp.stack([cos.T, sin.T], axis=0)


def _layer_weights(l, attn_norm, w_in, w_gate_up, b_gate, gla_norm, w_sconv, w_out, ffn_norm, w_up, w_ffn_conv, w_down):
    D = w_in.shape[1]
    wi = w_in[l]
    cuts = [0, 128, 256, 512, 528, 784, 1040, 1296, 1552, 2064, 2576, 3088, 3344, 3408, 3412]
    gq, gk, gv, glr, gr, sb, sc, sx, aq, ak, av, iq, ik, iw = [wi[:, cuts[i]:cuts[i + 1]] for i in range(14)]
    pad = jnp.zeros((D, LANE - GLA_RANK - IDX_HEADS), F32)
    wp = jnp.concatenate([gq, gk, gv, gr, sb, sc, sx, aq, iq, glr, iw, pad], axis=1).astype(BF16)
    wt = jnp.concatenate([ak, av, ik], axis=1).T.astype(BF16)
    wg = jnp.zeros((LANE, GLA_QK), F32).at[MISC_GLR:MISC_GLR + GLA_RANK].set(w_gate_up[l])
    d_ff = w_down.shape[1]
    n_ff = d_ff // FF_CHUNK
    wu = w_up[l].astype(BF16)
    wua = wu[:, :d_ff].reshape(D, n_ff, FF_CHUNK).transpose(1, 0, 2)
    wub = wu[:, d_ff:].reshape(D, n_ff, FF_CHUNK).transpose(1, 0, 2)
    wfc = w_ffn_conv[l].reshape(3, n_ff, FF_CHUNK).transpose(1, 0, 2)
    wdn = w_down[l].astype(BF16).reshape(n_ff, FF_CHUNK, D)
    return dict(nw=attn_norm[l][None], wp=wp, wt=wt, wg=wg, bg=b_gate[l][None], gn=gla_norm[l][None], wsc=w_sconv[l],
                wout=w_out[l].astype(BF16), fnw=ffn_norm[l][None], wua=wua, wub=wub, wfc=wfc, wdn=wdn)


def _unchunk(a):
    a = jnp.swapaxes(a, -3, -2)
    return a.reshape(a.shape[:-2] + (a.shape[-2] * a.shape[-1],))


def kernel(x_prompt, x_sample, cache_k, cache_v, cache_idx_k, page_table, state_gla, state_sconv, state_ffn_conv,
           meta_tokens, attn_norm, w_in, w_gate_up, b_gate, gla_norm, w_sconv, w_out, ffn_norm, w_up, w_ffn_conv,
           w_down, final_norm):
    B, S_len, D = x_prompt.shape
    depth = w_in.shape[0]
    T = S_len + N_META
    pad = (-T) % LANE
    R = pad + T
    DB, S = x_sample.shape[:2]
    n_pages = page_table.shape[1]
    past = n_pages * PAGE
    topk_p = min(TOPK_MAX, T // 4)
    topk_s = min(TOPK_MAX, (past + S) // 4)
    d_ff = w_down.shape[1]
    n_ff = d_ff // FF_CHUNK
    n_pool = cache_k.shape[1]
    assert S <= QPAD and (DB * S) % 8 == 0

    hp = jnp.concatenate([jnp.zeros((B, pad, D), F32), jnp.broadcast_to(meta_tokens[None], (B, N_META, D)), x_prompt],
                         axis=1)
    hs = x_sample.reshape(1, DB * S, D)
    rope_p, rope_pt = _rope_tables(jnp.maximum(jnp.arange(R) - pad, 0))
    rope_s, rope_st = _rope_tables(past + jnp.arange(DB * S) % S)
    ck = jnp.transpose(cache_k, (0, 1, 3, 4, 2)).reshape(depth, n_pool, ATT_W, PAGE)
    cv = jnp.transpose(cache_v, (0, 1, 3, 4, 2)).reshape(depth, n_pool, ATT_W, PAGE)
    cik = jnp.transpose(cache_idx_k, (0, 1, 3, 2))

    outs_p = [[] for _ in range(6)]
    outs_s = [[] for _ in range(6)]
    for l in range(depth):
        W = _layer_weights(l, attn_norm, w_in, w_gate_up, b_gate, gla_norm, w_sconv, w_out, ffn_norm, w_up,
                           w_ffn_conv, w_down)
        post_w = (W["wsc"], W["wout"], W["fnw"], W["wua"], W["wub"], W["wfc"], W["wdn"])

        proj_w = (W["nw"], W["wp"], W["wt"], W["wg"], W["bg"])
        gla, g, scv, q, iq, misc, kt, vt, ikt, ktb, vtb, iktb = _proj(hp, *proj_w, rope_p, rope_pt)
        o_gla, st_t = _gla_prompt(gla, g, W["gn"])
        o_att = _dsa_prompt(iq, misc, q, iktb, ktb, vtb, pad, topk_p)
        hp, st_s, st_f = _post(hp, o_gla, scv, o_att, *post_w)

        def heads_last(a):
            return a[:, :, pad:].reshape(B, ATT_HEADS, HEAD_DIM, T).transpose(0, 3, 1, 2)

        outs_p[0].append(heads_last(kt))
        outs_p[1].append(heads_last(vt))
        outs_p[2].append(ikt[:, :, pad:].transpose(0, 2, 1))
        outs_p[3].append(jnp.swapaxes(st_t, -1, -2))
        outs_p[4].append(st_s)
        outs_p[5].append(_unchunk(st_f))

        gla, g, scv, q, iq, misc, kt, vt, ikt, _, _, _ = _proj(hs, *proj_w, rope_s, rope_st)
        k, v, ik = kt[0].T, vt[0].T, ikt[0].T
        z = gla[0].reshape(DB, S, 768)
        tr = lambda a: jnp.swapaxes(a, 1, 2)
        eT = tr(g[0].reshape(DB, S, GLA_QK))
        qT = tr(z[:, :, 0:GLA_QK])
        kT = tr(z[:, :, GLA_QK:2 * GLA_QK])
        vx = jnp.repeat(z[:, :, 2 * GLA_QK:2 * GLA_QK + GLA_W].reshape(DB, S, GLA_HEADS, 1, GLA_DV), GLA_DK, axis=3)
        vx = vx.reshape(DB, S, GLA_QK, GLA_DV)
        gr = z[:, :, 2 * GLA_QK + GLA_W:]
        s0 = state_gla[l].reshape(DB, GLA_QK, GLA_DV)
        o_gla, s_new = _gla_sample(qT, kT, eT, vx, gr, W["gn"], s0)

        def hq(a, n_h, d):
            a = a.reshape(DB, S, n_h, d).transpose(0, 2, 1, 3)
            a = jnp.pad(a, ((0, 0), (0, 0), (0, QPAD - S), (0, 0)))
            return a.reshape(DB, n_h * QPAD, d)

        m0 = misc[0]
        iq_s = hq(iq[0], IDX_HEADS, IDX_DIM)
        w_s = hq(m0[:, MISC_IW:MISC_IW + IDX_HEADS], IDX_HEADS, 1)

        def new_page(a):
            a = a[0].reshape(a.shape[1], DB, S).transpose(1, 0, 2)
            return jnp.pad(a, ((0, 0), (0, 0), (0, PAGE - S)))

        keys, thr, js = _idx_sample(page_table, iq_s, w_s, new_page(ikt), cik, l, topk_s)
        q_s = jnp.pad(q[0].reshape(DB, S, ATT_W), ((0, 0), (0, QPAD - S), (0, 0)))
        o_att = _att_sample(page_table, q_s, keys, thr, js, new_page(kt), new_page(vt), ck, cv, l)[:, :S]

        def fix(prev, shift):
            p = jnp.zeros((DB, S) + prev.shape[2:], F32)
            for s in range(shift):
                p = p.at[:, s].set(prev[:, 2 - shift + s])
            return p.reshape((DB * S,) + prev.shape[2:])

        def chunked(a):
            return a.reshape(DB * S, n_ff, FF_CHUNK).transpose(1, 0, 2)

        sfix = (fix(state_sconv[l], 1), fix(state_sconv[l], 2),
                chunked(fix(state_ffn_conv[l], 1)), chunked(fix(state_ffn_conv[l], 2)))
        hs, u, a = _post(hs, o_gla.reshape(1, DB * S, GLA_W), scv, o_att.reshape(1, DB * S, ATT_W), *post_w,
                         sample_fix=sfix, n_tok=S)
        outs_s[0].append(k.reshape(DB, S, ATT_HEADS, HEAD_DIM))
        outs_s[1].append(v.reshape(DB, S, ATT_HEADS, HEAD_DIM))
        outs_s[2].append(ik.reshape(DB, S, IDX_DIM))
        outs_s[3].append(s_new.reshape(DB, GLA_HEADS, GLA_DK, GLA_DV))
        outs_s[4].append(u[0].reshape(DB, S, SCONV_W)[:, S - 2:])
        outs_s[5].append(_unchunk(a).reshape(DB, S, d_ff)[:, S - 2:])

    fw = final_norm[None]
    y_prompt = _final_norm(hp, fw, pad + N_META, S_len)
    y_sample = _final_norm(hs, fw, 0, DB * S).reshape(DB, S, D)
    sp = [jnp.stack(a, axis=0) for a in outs_p]
    ss = [jnp.stack(a, axis=0) for a in outs_s]
    return (y_prompt, y_sample, *sp, *ss)
```

```python
import functools

import jax
import jax.numpy as jnp
from jax import lax
from jax.experimental import pallas as pl
from jax.experimental.pallas import tpu as pltpu

F32 = jnp.float32
BF16 = jnp.bfloat16
I32 = jnp.int32
I16 = jnp.int16

N_META = 16
GLA_HEADS, GLA_DK, GLA_DV = 4, 32, 64
GLA_RANK = 16
GLA_TAU = 16.0
GLA_W = GLA_HEADS * GLA_DV
GLA_QK = GLA_HEADS * GLA_DK
SCONV_W = 256
ATT_HEADS, HEAD_DIM = 8, 64
ATT_W = ATT_HEADS * HEAD_DIM
ROT_DIM = 16
ROPE_THETA = 500000.0
IDX_HEADS, IDX_DIM = 4, 64
IDX_W = IDX_HEADS * IDX_DIM
TOPK_MAX = 256
PAGE = 128
EPS = 1e-6
FF_CHUNK = 256

LANE = 128
QB = 128
GLA_CHUNK = 64
NEG_BIG = -1e30
VMEM_LIMIT = 56 * 1024 * 1024

SEG_GLA = (0, 768)
SEG_SCV = (768, 1536)
SEG_AQ = (1536, 2048)
SEG_IQ = (2048, 2304)
SEG_MISC = (2304, 2432)
N_PROJ = 2432
MISC_GLR = 0
MISC_IW = 16
TSEG_K = (0, 512)
TSEG_V = (512, 1024)
TSEG_IK = (1024, 1088)
N_PROJ_T = 1088

INT_MIN = -2147483648
NEG_INF_KEY = -2139095041


def _tile(n, cands=(512, 384, 256, 128, 64, 32, 16, 8)):
    for c in cands:
        if n % c == 0:
            return c
    raise ValueError(f"no tile for {n}")


def _dot(a, b):
    return jnp.dot(a, b, preferred_element_type=F32)


def _dot_nt(a, b):
    return lax.dot_general(a, b, (((1,), (1,)), ((), ())), preferred_element_type=F32)


def _dot_tn(a, b):
    return lax.dot_general(a, b, (((0,), (0,)), ((), ())), preferred_element_type=F32)


def _cparams(n_axes):
    return pltpu.CompilerParams(dimension_semantics=("arbitrary",) * n_axes, vmem_limit_bytes=VMEM_LIMIT)


def _const_spec(shape):
    nd = len(shape)
    return pl.BlockSpec(shape, lambda *_: (0,) * nd, pipeline_mode=pl.Buffered(1))


def _to_key(x):
    bits = pltpu.bitcast(x, I32)
    return bits ^ ((bits >> 31) & 0x7FFFFFFF)


def _proj_kernel(x_ref, nw_ref, w_ref, wt_ref, wg_ref, bg_ref, rope_ref, ropet_ref,
                 gla_ref, g_ref, scv_ref, q_ref, iq_ref, misc_ref,
                 kt_ref, vt_ref, ikt_ref, ktb_ref, vtb_ref, iktb_ref):
    x = x_ref[...]
    xn = x * lax.rsqrt(jnp.mean(x * x, axis=-1, keepdims=True) + EPS) * nw_ref[...]
    xb = xn.astype(BF16)

    def seg(s):
        return _dot(xb, w_ref[:, s[0]:s[1]])

    def seg_t(s):
        return _dot_nt(wt_ref[s[0]:s[1], :], xb)

    cos, s1, s2 = rope_ref[0], rope_ref[1], rope_ref[2]

    def rope128(z, c, a, b):
        return z * c + pltpu.roll(z, LANE - ROT_DIM // 2, 1) * a + pltpu.roll(z, ROT_DIM // 2, 1) * b

    gla_ref[...] = seg(SEG_GLA)
    scv_ref[...] = seg(SEG_SCV)

    zq = seg(SEG_AQ)
    for c in range(ATT_W // LANE):
        sl = slice(c * LANE, (c + 1) * LANE)
        q_ref[:, sl] = (rope128(zq[:, sl], cos, s1, s2) * HEAD_DIM ** -0.5).astype(BF16)
    zi = seg(SEG_IQ)
    for c in range(IDX_W // LANE):
        sl = slice(c * LANE, (c + 1) * LANE)
        iq_ref[:, sl] = (rope128(zi[:, sl], cos, s1, s2) * IDX_DIM ** -0.5).astype(BF16)

    zm = seg(SEG_MISC)
    gpre = jnp.dot(zm, wg_ref[...], preferred_element_type=F32, precision=lax.Precision.HIGHEST) + bg_ref[...]
    g_ref[...] = jax.nn.log_sigmoid(gpre) * (1.0 / GLA_TAU)
    lane = lax.broadcasted_iota(I32, zm.shape, 1)
    misc_ref[...] = zm * jnp.where((lane >= MISC_IW) & (lane < MISC_IW + IDX_HEADS), IDX_HEADS ** -0.5, 1.0)

    cos_t, sin_t = ropet_ref[0], ropet_ref[1]
    half = ROT_DIM // 2

    def store_roped_t(z, n_heads, ref32, ref16):
        for h in range(n_heads):
            r0 = h * HEAD_DIM
            x1, x2 = z[r0:r0 + half], z[r0 + half:r0 + 2 * half]
            parts = ((r0, x1 * cos_t - x2 * sin_t), (r0 + half, x2 * cos_t + x1 * sin_t),
                     (r0 + 2 * half, z[r0 + 2 * half:r0 + HEAD_DIM]))
            for start, val in parts:
                ref32[start:start + val.shape[0], :] = val
        ref16[...] = ref32[...].astype(BF16)

    store_roped_t(seg_t(TSEG_K), ATT_HEADS, kt_ref, ktb_ref)
    zv = seg_t(TSEG_V)
    vt_ref[...] = zv
    vtb_ref[...] = zv.astype(BF16)
    store_roped_t(seg_t(TSEG_IK), 1, ikt_ref, iktb_ref)


def _proj(x, nw, w, wt, wg, bg, rope, rope_t):
    B, R, D = x.shape
    tm = _tile(R)
    grid = (B, R // tm)

    def rspec(width):
        return pl.BlockSpec((None, tm, width), lambda b, t: (b, t, 0))

    def tspec(rows):
        return pl.BlockSpec((None, rows, tm), lambda b, t: (b, 0, t))

    outs = [(768, F32), (GLA_QK, F32), (768, F32), (ATT_W, BF16), (IDX_W, BF16), (LANE, F32)]
    outs_t = [(ATT_W, F32), (ATT_W, F32), (IDX_DIM, F32), (ATT_W, BF16), (ATT_W, BF16), (IDX_DIM, BF16)]
    return pl.pallas_call(
        _proj_kernel,
        grid=grid,
        in_specs=[rspec(D), _const_spec((1, D)), _const_spec((D, N_PROJ)), _const_spec((N_PROJ_T, D)),
                  _const_spec((LANE, GLA_QK)), _const_spec((1, GLA_QK)),
                  pl.BlockSpec((3, tm, LANE), lambda b, t: (0, t, 0)),
                  pl.BlockSpec((2, ROT_DIM // 2, tm), lambda b, t: (0, 0, t))],
        out_specs=[rspec(wd) for wd, _ in outs] + [tspec(r) for r, _ in outs_t],
        out_shape=[jax.ShapeDtypeStruct((B, R, wd), dt) for wd, dt in outs]
        + [jax.ShapeDtypeStruct((B, r, R), dt) for r, dt in outs_t],
        compiler_params=_cparams(2),
        name="proj",
    )(x, nw, w, wt, wg, bg, rope, rope_t)


def _gla_kernel(gla_ref, g_ref, gn_ref, o_ref, st_ref, s_sc):
    t = pl.program_id(1)
    C = GLA_CHUNK

    @pl.when(t == 0)
    def _():
        s_sc[...] = jnp.zeros_like(s_sc)

    z = gla_ref[...]
    g = g_ref[...]
    r_i = lax.broadcasted_iota(I32, (C, C), 0)
    c_i = lax.broadcasted_iota(I32, (C, C), 1)
    causal = c_i <= r_i
    tri = jnp.where(causal, 1.0, 0.0).astype(F32)
    b = jnp.dot(tri, g, preferred_element_type=F32, precision=lax.Precision.HIGHEST)
    b_last = b[C - 1:C, :]
    b_mid = b[C // 2:C // 2 + 1, :]
    e_in = jnp.exp(b)
    e_q = jnp.exp(b - b_mid)
    e_k = jnp.exp(b_mid - b)
    e_out = jnp.exp(b_last - b)
    e_last = jnp.exp(b_last)
    qz = z[:, 0:GLA_QK] * GLA_DK ** -0.5
    kz = z[:, GLA_QK:2 * GLA_QK]
    q_in = (qz * e_in).astype(BF16)
    q_x = (qz * e_q).astype(BF16)
    k_x = (kz * e_k).astype(BF16)
    k_out = (kz * e_out).astype(BF16)
    gn = gn_ref[...]
    for h in range(GLA_HEADS):
        ks = slice(h * GLA_DK, (h + 1) * GLA_DK)
        v = z[:, 2 * GLA_QK + h * GLA_DV:2 * GLA_QK + (h + 1) * GLA_DV]
        gr = z[:, 2 * GLA_QK + GLA_W + h * GLA_DV:2 * GLA_QK + GLA_W + (h + 1) * GLA_DV]
        vb = v.astype(BF16)
        st = s_sc[h]
        att = jnp.where(causal, _dot_nt(q_x[:, ks], k_x[:, ks]), 0.0)
        o = _dot(att.astype(BF16), vb) + _dot_nt(q_in[:, ks], st.astype(BF16))
        s_sc[h] = st * e_last[:, ks] + _dot_tn(vb, k_out[:, ks])
        on = o * lax.rsqrt(jnp.mean(o * o, axis=-1, keepdims=True) + EPS) * gn
        o_ref[:, h * GLA_DV:(h + 1) * GLA_DV] = on * jax.nn.silu(gr)
    st_ref[...] = s_sc[...]


def _gla_prompt(gla, g, gn):
    B, R, _ = gla.shape
    C = GLA_CHUNK
    return pl.pallas_call(
        _gla_kernel,
        grid=(B, R // C),
        in_specs=[pl.BlockSpec((None, C, 768), lambda b, t: (b, t, 0)),
                  pl.BlockSpec((None, C, GLA_QK), lambda b, t: (b, t, 0)),
                  _const_spec((1, GLA_DV))],
        out_specs=[pl.BlockSpec((None, C, GLA_W), lambda b, t: (b, t, 0)),
                   pl.BlockSpec((None, GLA_HEADS, GLA_DV, GLA_DK), lambda b, t: (b, 0, 0, 0))],
        out_shape=[jax.ShapeDtypeStruct((B, R, GLA_W), F32),
                   jax.ShapeDtypeStruct((B, GLA_HEADS, GLA_DV, GLA_DK), F32)],
        scratch_shapes=[pltpu.VMEM((GLA_HEADS, GLA_DV, GLA_DK), F32)],
        compiler_params=_cparams(2),
        name="gla_prompt",
    )(gla, g, gn)


def _gla_step_kernel(qT_ref, kT_ref, gT_ref, vx_ref, gr_ref, gn_ref, s0_ref, o_ref, s_ref, *, n_tok):
    S = s0_ref[...]
    qT, kT, eT = qT_ref[...] * GLA_DK ** -0.5, kT_ref[...], jnp.exp(gT_ref[...])
    gn = gn_ref[...]
    gr = gr_ref[...]
    for t in range(n_tok):
        S = eT[:, t:t + 1] * S + kT[:, t:t + 1] * vx_ref[t]
        qs = qT[:, t:t + 1] * S
        for h in range(GLA_HEADS):
            o = jnp.sum(qs[h * GLA_DK:(h + 1) * GLA_DK, :], axis=0, keepdims=True)
            on = o * lax.rsqrt(jnp.mean(o * o, axis=-1, keepdims=True) + EPS) * gn
            hs = slice(h * GLA_DV, (h + 1) * GLA_DV)
            o_ref[t:t + 1, hs] = on * jax.nn.silu(gr[t:t + 1, hs])
    s_ref[...] = S


def _gla_sample(qT, kT, eT, vx, gr, gn, s0):
    DB, HK, n_tok = qT.shape

    def bspec(*shape):
        nd = len(shape)
        return pl.BlockSpec((None,) + shape, lambda b: (b,) + (0,) * nd)

    return pl.pallas_call(
        functools.partial(_gla_step_kernel, n_tok=n_tok),
        grid=(DB,),
        in_specs=[bspec(HK, n_tok), bspec(HK, n_tok), bspec(HK, n_tok), bspec(n_tok, HK, GLA_DV),
                  bspec(n_tok, GLA_W), _const_spec((1, GLA_DV)), bspec(HK, GLA_DV)],
        out_specs=[bspec(n_tok, GLA_W), bspec(HK, GLA_DV)],
        out_shape=[jax.ShapeDtypeStruct((DB, n_tok, GLA_W), F32), jax.ShapeDtypeStruct((DB, HK, GLA_DV), F32)],
        compiler_params=_cparams(1),
        name="gla_sample",
    )(qT, kT, eT, vx, gr, gn, s0)


def _chunk_loop(n_chunks, body, init):
    return body(0, init) if isinstance(n_chunks, int) and n_chunks == 1 else lax.fori_loop(0, n_chunks, body, init)


def _count_cols(key_ref, rsl, rows, n_chunks, cw, pred):
    dt = key_ref.dtype

    def body(c, acc):
        off = pl.multiple_of(c * cw, cw)
        col = off + lax.broadcasted_iota(I32, (rows, LANE), 1)
        for j in range(cw // LANE):
            k = key_ref[rsl, pl.ds(off + j * LANE, LANE)]
            acc = acc + jnp.where(pred(k, col + j * LANE), jnp.ones((), dt), jnp.zeros((), dt))
        return acc

    return _chunk_loop(n_chunks, body, jnp.zeros((rows, LANE), dt))


def _selected(key, col, thr, js):
    return (key - jnp.where(col >= js, 1, 0)) >= thr


def _select_threshold(key_ref, half_ref, thr_ref, js_ref, n_groups, rows, n_chunks, cw, topk, idx_bits):
    gs = range(n_groups)
    rsl = [slice(g * rows, (g + 1) * rows) for g in gs]
    n_slab = cw // LANE
    i16_min, i16_max = -32768, 32767

    def count(g, pred):
        return jnp.sum(_count_cols(key_ref, rsl[g], rows, n_chunks, cw, pred), axis=1, keepdims=True)

    def count16(g, cand):
        c16 = cand.astype(I16)
        return _count_cols(half_ref, rsl[g], rows, n_chunks, cw, lambda k, col: k >= c16)

    def total(part):
        return jnp.sum(part.astype(I32), axis=1, keepdims=True)

    def bisect16():
        def start(g):
            return jnp.where(total(count16(g, jnp.zeros((rows, 1), I32))) >= topk, 0, i16_min).astype(I32)

        def bs(it, los):
            bit = lax.shift_left(jnp.int32(1), 14 - it)
            cands = [los[g] + bit for g in gs]
            parts = [count16(g, cands[g]) for g in gs]
            return tuple(jnp.where(total(parts[g]) >= topk, cands[g], los[g]) for g in gs)

        return lax.fori_loop(0, 15, bs, tuple(start(g) for g in gs))

    def fill_half(g, fn):
        def body(c, carry):
            off = pl.multiple_of(c * cw, cw)
            for j in range(n_slab):
                sl = pl.ds(off + j * LANE, LANE)
                half_ref[rsl[g], sl] = fn(key_ref[rsl[g], sl]).astype(I16)
            return carry

        _chunk_loop(n_chunks, body, 0)

    for g in gs:
        fill_half(g, lambda k: k >> 16)
    t_hi = bisect16()
    for g in gs:
        t = t_hi[g]

        def low_or_sentinel(k, t=t):
            hi = k >> 16
            low = (k & 0xFFFF) + i16_min
            return jnp.where(hi > t, i16_max, jnp.where(hi == t, low, i16_min))

        fill_half(g, low_or_sentinel)
    t_lo = bisect16()
    thrs = [lax.shift_left(t_hi[g], 16) | ((t_lo[g] - i16_min) & 0xFFFF) for g in gs]

    for g in gs:
        thr = thrs[g]
        n_gt = count(g, lambda k, col: k > thr)
        n_ge = count(g, lambda k, col: k >= thr)
        need = topk - n_gt
        excess = (n_ge > topk) & (thr > NEG_INF_KEY)
        base = jnp.where(thr > NEG_INF_KEY, 1 << idx_bits, 0).astype(I32)
        thr_ref[rsl[g], :] = jnp.broadcast_to(thr, (rows, LANE))
        js_ref[rsl[g], :] = jnp.broadcast_to(base, (rows, LANE))

        @pl.when(jnp.max(jnp.where(excess, 1, 0)) > 0)
        def _(g=g, thr=thr, need=need, excess=excess, base=base):
            def bs2(it, j):
                cand = j + lax.shift_left(jnp.int32(1), idx_bits - 1 - it)
                c = count(g, lambda k, col: (k == thr) & (col < cand))
                return jnp.where(c < need, cand, j)

            j = lax.fori_loop(0, idx_bits, bs2, jnp.zeros((rows, 1), I32))
            js_ref[rsl[g], :] = jnp.broadcast_to(jnp.where(excess, j + 1, base), (rows, LANE))


def _dsa_prompt_kernel(iq_ref, misc_ref, q_ref, ikt_ref, kt_ref, vt_ref, o_ref,
                       key_sc, half_sc, thr_sc, js_sc, mp_sc, lp_sc, acc_sc, *, pad, topk, qb, idx_bits):
    i = pl.program_id(1)
    kc = qb
    n_kc = i + 1
    n_slab = kc // LANE
    row = i * qb + lax.broadcasted_iota(I32, (qb, 1), 0)

    def cols(c):
        return pl.ds(pl.multiple_of(c * kc, kc), kc)

    misc = misc_ref[...]
    iq = iq_ref[...]
    iqh = [iq[:, h * IDX_DIM:(h + 1) * IDX_DIM] for h in range(IDX_HEADS)]
    ws = [misc[:, MISC_IW + h:MISC_IW + h + 1] for h in range(IDX_HEADS)]

    def p1(c, carry):
        ikt = ikt_ref[:, cols(c)]
        sc = ws[0] * jnp.maximum(_dot(iqh[0], ikt), 0.0)
        for h in range(1, IDX_HEADS):
            sc = sc + ws[h] * jnp.maximum(_dot(iqh[h], ikt), 0.0)
        col = c * kc + lax.broadcasted_iota(I32, (qb, kc), 1)
        sc = jnp.where((col <= row) & (col >= pad), sc + 0.0, -jnp.inf)
        key_sc[:, cols(c)] = _to_key(sc)
        return carry

    lax.fori_loop(0, n_kc, p1, 0)

    n_groups = 4
    _select_threshold(key_sc, half_sc, thr_sc, js_sc, n_groups, qb // n_groups, n_kc, kc, topk, idx_bits)
    thr = thr_sc[:, 0:1]
    js = js_sc[:, 0:1]

    def pmask(c, carry):
        col = c * kc + lax.broadcasted_iota(I32, (qb, kc), 1)
        sel = _selected(key_sc[:, cols(c)], col, thr, js)
        key_sc[:, cols(c)] = pltpu.bitcast(jnp.where(sel, 0.0, NEG_BIG), I32)
        return carry

    lax.fori_loop(0, n_kc, pmask, 0)

    q = q_ref[...]
    qh = [q[:, h * HEAD_DIM:(h + 1) * HEAD_DIM] for h in range(ATT_HEADS)]

    def scores(c, h):
        hs = slice(h * HEAD_DIM, (h + 1) * HEAD_DIM)
        return _dot(qh[h], kt_ref[hs, cols(c)]) + pltpu.bitcast(key_sc[:, cols(c)], F32)

    mp_sc[...] = jnp.full(mp_sc.shape, NEG_BIG, F32)

    def pa(c, carry):
        for h in range(ATT_HEADS):
            s = scores(c, h)
            mp = mp_sc[h]
            for j in range(n_slab):
                mp = jnp.maximum(mp, s[:, j * LANE:(j + 1) * LANE])
            mp_sc[h] = mp
        return carry

    lax.fori_loop(0, n_kc, pa, 0)
    for h in range(ATT_HEADS):
        mp_sc[h] = jnp.broadcast_to(jnp.max(mp_sc[h], axis=1, keepdims=True), (qb, LANE))

    lp_sc[...] = jnp.zeros_like(lp_sc)
    acc_sc[...] = jnp.zeros_like(acc_sc)

    def pb(c, carry):
        for h in range(ATT_HEADS):
            hs = slice(h * HEAD_DIM, (h + 1) * HEAD_DIM)
            s = scores(c, h)
            m = mp_sc[h]
            ps = [jnp.exp(s[:, j * LANE:(j + 1) * LANE] - m) for j in range(n_slab)]
            lp = lp_sc[h]
            for pj in ps:
                lp = lp + pj
            lp_sc[h] = lp
            p = jnp.concatenate(ps, axis=1).astype(BF16)
            acc_sc[h] += _dot_nt(p, vt_ref[hs, cols(c)])
        return carry

    lax.fori_loop(0, n_kc, pb, 0)

    valid = row >= pad
    for h in range(ATT_HEADS):
        l = jnp.sum(lp_sc[h], axis=1, keepdims=True)
        o_ref[:, h * HEAD_DIM:(h + 1) * HEAD_DIM] = jnp.where(valid, acc_sc[h] / l, 0.0)


def _dsa_prompt(iq, misc, q, iktb, ktb, vtb, pad, topk):
    B, R, _ = q.shape
    qb = _tile(R, (384, 256, 128))
    assert qb >= topk
    idx_bits = max(1, (R - 1).bit_length())

    def qspec(width):
        return pl.BlockSpec((None, qb, width), lambda b, i: (b, i, 0))

    def fspec(rows):
        return pl.BlockSpec((None, rows, R), lambda b, i: (b, 0, 0))

    return pl.pallas_call(
        functools.partial(_dsa_prompt_kernel, pad=pad, topk=topk, qb=qb, idx_bits=idx_bits),
        grid=(B, R // qb),
        in_specs=[qspec(IDX_W), qspec(LANE), qspec(ATT_W), fspec(IDX_DIM), fspec(ATT_W), fspec(ATT_W)],
        out_specs=qspec(ATT_W),
        out_shape=jax.ShapeDtypeStruct((B, R, ATT_W), F32),
        scratch_shapes=[pltpu.VMEM((qb, R), I32), pltpu.VMEM((qb, R), I16),
                        pltpu.VMEM((qb, LANE), I32), pltpu.VMEM((qb, LANE), I32),
                        pltpu.VMEM((ATT_HEADS, qb, LANE), F32), pltpu.VMEM((ATT_HEADS, qb, LANE), F32),
                        pltpu.VMEM((ATT_HEADS, qb, HEAD_DIM), F32)],
        compiler_params=_cparams(2),
        name="dsa_prompt",
    )(iq, misc, q, iktb, ktb, vtb)


QPAD = 8


def _idx_sample_kernel(pt_ref, iq_ref, w_ref, iknt_ref, *rest, pg, n_pg, past):
    pages = rest[:pg]
    key_ref = rest[pg]
    g = pl.program_id(1)
    iq = iq_ref[...]
    w = w_ref[...]

    def score(ikt):
        r = jnp.maximum(_dot(iq, ikt), 0.0) * w
        s = r[0:QPAD]
        for h in range(1, IDX_HEADS):
            s = s + r[h * QPAD:(h + 1) * QPAD]
        return s + 0.0

    ikt = jnp.concatenate([p[...].astype(BF16) for p in pages], axis=1)
    off = pl.multiple_of(g * (pg * PAGE), pg * PAGE)
    key_ref[:, pl.ds(off, pg * PAGE)] = _to_key(score(ikt))

    @pl.when(g == n_pg - 1)
    def _():
        sn = score(iknt_ref[...].astype(BF16))
        qrow = lax.broadcasted_iota(I32, (QPAD, PAGE), 0)
        colr = lax.broadcasted_iota(I32, (QPAD, PAGE), 1)
        key_ref[:, past:past + PAGE] = _to_key(jnp.where(colr <= qrow, sn, -jnp.inf))


def _idx_sample(page_table, iq, w, iknt, cache_ikt, layer):
    DB, n_pages = page_table.shape
    past = n_pages * PAGE
    pg = 16 if n_pages % 16 == 0 else (4 if n_pages % 4 == 0 else 1)
    n_pg = n_pages // pg
    ncol = past + PAGE
    rows = IDX_HEADS * QPAD

    def page_spec(j):
        return pl.BlockSpec((None, None, IDX_DIM, PAGE), lambda b, g, pt: (layer, pt[b, g * pg + j], 0, 0))

    def bspec(*shape):
        nd = len(shape)
        return pl.BlockSpec((None,) + shape, lambda b, g, pt: (b,) + (0,) * nd)

    grid_spec = pltpu.PrefetchScalarGridSpec(
        num_scalar_prefetch=1,
        grid=(DB, n_pg),
        in_specs=[bspec(rows, IDX_DIM), bspec(rows, 1), bspec(IDX_DIM, PAGE)] + [page_spec(j) for j in range(pg)],
        out_specs=bspec(QPAD, ncol),
    )
    return pl.pallas_call(
        functools.partial(_idx_sample_kernel, pg=pg, n_pg=n_pg, past=past),
        grid_spec=grid_spec,
        out_shape=jax.ShapeDtypeStruct((DB, QPAD, ncol), I32),
        compiler_params=_cparams(2),
        name="idx_sample",
    )(page_table, iq, w, iknt, *([cache_ikt] * pg))


def _thr_sample_kernel(key_ref, thr_ref, js_ref, half_sc, *, n_groups, n_chunks, cw, topk, idx_bits):
    _select_threshold(key_ref, half_sc, thr_ref, js_ref, n_groups, key_ref.shape[0] // n_groups, n_chunks, cw, topk,
                      idx_bits)


def _thr_sample(keys, topk):
    n_rows, ncol = keys.shape
    rb = _tile(n_rows, (64, 32, 16))
    n_groups = 2 if rb % 32 == 0 else 1
    n_slab = ncol // LANE
    per_chunk = max(d for d in range(1, 65) if n_slab % d == 0)
    return pl.pallas_call(
        functools.partial(_thr_sample_kernel, n_groups=n_groups, n_chunks=n_slab // per_chunk, cw=per_chunk * LANE,
                          topk=topk, idx_bits=max(1, (ncol - 1).bit_length())),
        grid=(n_rows // rb,),
        in_specs=[pl.BlockSpec((rb, ncol), lambda i: (i, 0))],
        out_specs=[pl.BlockSpec((rb, LANE), lambda i: (i, 0)), pl.BlockSpec((rb, LANE), lambda i: (i, 0))],
        out_shape=[jax.ShapeDtypeStruct((n_rows, LANE), I32), jax.ShapeDtypeStruct((n_rows, LANE), I32)],
        scratch_shapes=[pltpu.VMEM((rb, ncol), I16)],
        compiler_params=_cparams(1),
        name="thr_sample",
    )(keys)


def _att_sample_kernel(pt_ref, q_ref, keyp_ref, keyn_ref, thr_ref, js_ref, kn_ref, vn_ref, *rest, pg, n_pg, past):
    kpages = rest[:pg]
    vpages = rest[pg:2 * pg]
    o_ref = rest[2 * pg]
    m_sc, l_sc, acc_sc = rest[2 * pg + 1:]
    g = pl.program_id(1)
    rows = ATT_HEADS * QPAD

    @pl.when(g == 0)
    def _():
        m_sc[...] = jnp.full(m_sc.shape, NEG_BIG, F32)
        l_sc[...] = jnp.zeros_like(l_sc)
        acc_sc[...] = jnp.zeros_like(acc_sc)

    q = q_ref[...]
    r_h = lax.broadcasted_iota(I32, (rows, ATT_W), 0) >> 3
    l_h = lax.broadcasted_iota(I32, (rows, ATT_W), 1) >> 6
    own = r_h == l_h
    qbd = jnp.where(own, jnp.concatenate([q.astype(F32)] * ATT_HEADS, axis=0), 0.0).astype(BF16)
    thr = thr_ref[:, 0:1]
    js = js_ref[:, 0:1]

    def update(kts, vts, key, colbase):
        width = key.shape[1]
        col = colbase + lax.broadcasted_iota(I32, (QPAD, width), 1)
        bias8 = jnp.where(_selected(key, col, thr, js), 0.0, NEG_BIG)
        s = jnp.concatenate([_dot(qbd, kt.astype(BF16)) for kt in kts], axis=1)
        s = s + jnp.concatenate([bias8] * ATT_HEADS, axis=0)
        m_old = m_sc[...]
        m_new = jnp.maximum(m_old, jnp.max(s, axis=1, keepdims=True))
        alpha = jnp.exp(m_old - m_new)
        pe = jnp.exp(s - m_new[:, 0:1])
        l_sc[...] = alpha * l_sc[...] + jnp.sum(pe, axis=1, keepdims=True)
        m_sc[...] = m_new
        pv = _dot_nt(pe[:, 0:PAGE].astype(BF16), vts[0].astype(BF16))
        for j in range(1, len(vts)):
            pv = pv + _dot_nt(pe[:, j * PAGE:(j + 1) * PAGE].astype(BF16), vts[j].astype(BF16))
        acc_sc[...] = alpha[:, 0:1] * acc_sc[...] + pv

    update([p[...] for p in kpages], [p[...] for p in vpages], keyp_ref[...], g * (pg * PAGE))

    @pl.when(g == n_pg - 1)
    def _():
        update([kn_ref[...]], [vn_ref[...]], keyn_ref[...], past)
        res = jnp.where(own, acc_sc[...] / l_sc[:, 0:1], 0.0)
        o = res[0:QPAD]
        for h in range(1, ATT_HEADS):
            o = o + res[h * QPAD:(h + 1) * QPAD]
        o_ref[...] = o


def _att_sample(page_table, q, keys, thr, js, kn, vn, cache_k, cache_v, layer):
    DB, n_pages = page_table.shape
    past = n_pages * PAGE
    pg = 16 if n_pages % 16 == 0 else (4 if n_pages % 4 == 0 else 1)
    n_pg = n_pages // pg
    rows = ATT_HEADS * QPAD

    def kv_spec(j):
        return pl.BlockSpec((None, None, ATT_W, PAGE), lambda b, g, pt: (layer, pt[b, g * pg + j], 0, 0))

    def bspec(*shape):
        nd = len(shape)
        return pl.BlockSpec((None,) + shape, lambda b, g, pt: (b,) + (0,) * nd)

    grid_spec = pltpu.PrefetchScalarGridSpec(
        num_scalar_prefetch=1,
        grid=(DB, n_pg),
        in_specs=[bspec(QPAD, ATT_W),
                  pl.BlockSpec((None, QPAD, pg * PAGE), lambda b, g, pt: (b, 0, g)),
                  pl.BlockSpec((None, QPAD, PAGE), lambda b, g, pt: (b, 0, n_pages)),
                  bspec(QPAD, LANE), bspec(QPAD, LANE), bspec(ATT_W, PAGE), bspec(ATT_W, PAGE)]
        + [kv_spec(j) for j in range(pg)] + [kv_spec(j) for j in range(pg)],
        out_specs=bspec(QPAD, ATT_W),
        scratch_shapes=[pltpu.VMEM((rows, PAGE), F32), pltpu.VMEM((rows, PAGE), F32), pltpu.VMEM((rows, ATT_W), F32)],
    )
    return pl.pallas_call(
        functools.partial(_att_sample_kernel, pg=pg, n_pg=n_pg, past=past),
        grid_spec=grid_spec,
        out_shape=jax.ShapeDtypeStruct((DB, QPAD, ATT_W), F32),
        compiler_params=_cparams(2),
        name="att_sample",
    )(page_table, q, keys, keys, thr, js, kn, vn, *([cache_k] * pg), *([cache_v] * pg))


def _conv3(u, w, first, second):
    u1 = jnp.where(first[0], first[1], pltpu.roll(u, 1, 0))
    u2 = jnp.where(second[0], second[1], pltpu.roll(u, 2, 0))
    return w[0:1] * u2 + w[1:2] * u1 + w[2:3] * u


def _post_kernel(*refs, sample, n_tok, n_ff):
    if sample:
        (h_ref, og_ref, scv_ref, oa_ref, p1s_ref, p2s_ref, p1f_ref, p2f_ref, wsc_ref, wout_ref, fnw_ref, wua_ref,
         wub_ref, wfc_ref, wdn_ref, h2_ref, u_ref, a_ref, acc_sc, hn_sc) = refs
    else:
        (h_ref, og_ref, scv_ref, oa_ref, wsc_ref, wout_ref, fnw_ref, wua_ref, wub_ref, wfc_ref, wdn_ref,
         h2_ref, sts_ref, stf_ref, cs_sc, cf_sc, acc_sc, hn_sc) = refs
    tm = h_ref.shape[0]

    if not sample:
        @pl.when(pl.program_id(1) == 0)
        def _():
            cs_sc[...] = jnp.zeros_like(cs_sc)
            cf_sc[...] = jnp.zeros_like(cf_sc)

    def fixes(width, c0, c1, p1, p2):
        rows = lax.broadcasted_iota(I32, (tm, width), 0)
        if sample:
            sp = lax.rem(rows, n_tok)
            return (sp < 1, p1), (sp < 2, p2)
        return (rows == 0, c1), (rows < 2, jnp.where(rows == 0, c0, c1))

    scv = scv_ref[...]
    sb = scv[:, 0:SCONV_W]
    u = scv[:, SCONV_W:2 * SCONV_W] * scv[:, 2 * SCONV_W:3 * SCONV_W]
    if sample:
        f1, f2 = fixes(SCONV_W, None, None, p1s_ref[...], p2s_ref[...])
        u_ref[...] = u
    else:
        f1, f2 = fixes(SCONV_W, cs_sc[0:1, :], cs_sc[1:2, :], None, None)
        cs_sc[0:2, :] = u[tm - 2:tm, :]
        sts_ref[...] = u[tm - 2:tm, :]
    o_s = sb * _conv3(u, wsc_ref[...], f1, f2)

    mix = (_dot(og_ref[...].astype(BF16), wout_ref[0:GLA_W, :])
           + _dot(o_s.astype(BF16), wout_ref[GLA_W:GLA_W + SCONV_W, :])
           + _dot(oa_ref[...].astype(BF16), wout_ref[GLA_W + SCONV_W:, :]))
    h1 = h_ref[...] + mix
    hn_sc[...] = (h1 * lax.rsqrt(jnp.mean(h1 * h1, axis=-1, keepdims=True) + EPS) * fnw_ref[...]).astype(BF16)
    acc_sc[...] = h1

    def ff(c, carry):
        hn = hn_sc[...]
        a = _dot(hn, wua_ref[c])
        bgate = _dot(hn, wub_ref[c])
        if sample:
            f1, f2 = fixes(FF_CHUNK, None, None, p1f_ref[c], p2f_ref[c])
            a_ref[c] = a
        else:
            f1, f2 = fixes(FF_CHUNK, cf_sc[c, 0:1, :], cf_sc[c, 1:2, :], None, None)
            cf_sc[c, 0:2, :] = a[tm - 2:tm, :]
            stf_ref[c] = a[tm - 2:tm, :]
        ac = _conv3(a, wfc_ref[c], f1, f2)
        act = (jax.nn.silu(ac) * bgate).astype(BF16)
        acc_sc[...] += _dot(act, wdn_ref[c])
        return carry

    lax.fori_loop(0, n_ff, ff, 0)
    h2_ref[...] = acc_sc[...]


def _post(h, og, scv, oa, wsc, wout, fnw, wua, wub, wfc, wdn, sample_fix=None, n_tok=1):
    B, R, D = h.shape
    n_ff = wua.shape[0]
    sample = sample_fix is not None
    tm = _tile(R)

    def rspec(width):
        return pl.BlockSpec((None, tm, width), lambda b, t: (b, t, 0))

    in_specs = [rspec(D), rspec(GLA_W), rspec(768), rspec(ATT_W)]
    args = [h, og, scv, oa]
    if sample:
        assert B == 1 and R == tm
        p1s, p2s, p1f, p2f = sample_fix
        in_specs += [_const_spec((tm, SCONV_W)), _const_spec((tm, SCONV_W)),
                     _const_spec((n_ff, tm, FF_CHUNK)), _const_spec((n_ff, tm, FF_CHUNK))]
        args += [p1s, p2s, p1f, p2f]
        out_specs = [rspec(D), rspec(SCONV_W), pl.BlockSpec((n_ff, tm, FF_CHUNK), lambda b, t: (0, 0, 0))]
        out_shape = [jax.ShapeDtypeStruct((B, R, D), F32), jax.ShapeDtypeStruct((B, R, SCONV_W), F32),
                     jax.ShapeDtypeStruct((n_ff, R, FF_CHUNK), F32)]
        scratch = []
    else:
        out_specs = [rspec(D), pl.BlockSpec((None, 2, SCONV_W), lambda b, t: (b, 0, 0)),
                     pl.BlockSpec((None, n_ff, 2, FF_CHUNK), lambda b, t: (b, 0, 0, 0))]
        out_shape = [jax.ShapeDtypeStruct((B, R, D), F32), jax.ShapeDtypeStruct((B, 2, SCONV_W), F32),
                     jax.ShapeDtypeStruct((B, n_ff, 2, FF_CHUNK), F32)]
        scratch = [pltpu.VMEM((8, SCONV_W), F32), pltpu.VMEM((n_ff, 8, FF_CHUNK), F32)]
    in_specs += [_const_spec((3, SCONV_W)), _const_spec((D, D)), _const_spec((1, D)),
                 _const_spec((n_ff, D, FF_CHUNK)), _const_spec((n_ff, D, FF_CHUNK)),
                 _const_spec((n_ff, 3, FF_CHUNK)), _const_spec((n_ff, FF_CHUNK, D))]
    args += [wsc, wout, fnw, wua, wub, wfc, wdn]
    scratch += [pltpu.VMEM((tm, D), F32), pltpu.VMEM((tm, D), BF16)]
    return pl.pallas_call(
        functools.partial(_post_kernel, sample=sample, n_tok=n_tok, n_ff=n_ff),
        grid=(B, R // tm),
        in_specs=in_specs,
        out_specs=out_specs,
        out_shape=out_shape,
        scratch_shapes=scratch,
        compiler_params=_cparams(2),
        name="post_sample" if sample else "post_prompt",
    )(*args)


def _norm_kernel(x_ref, w_ref, o_ref):
    x = x_ref[...]
    o_ref[...] = x * lax.rsqrt(jnp.mean(x * x, axis=-1, keepdims=True) + EPS) * w_ref[...]


def _final_norm(h, w, skip_rows, n_rows):
    B, _, D = h.shape
    tm = LANE if n_rows % LANE == 0 else n_rows
    assert skip_rows % tm == 0 and n_rows % tm == 0
    off = skip_rows // tm
    return pl.pallas_call(
        _norm_kernel,
        grid=(B, n_rows // tm),
        in_specs=[pl.BlockSpec((None, tm, D), lambda b, t: (b, t + off, 0)), _const_spec((1, D))],
        out_specs=pl.BlockSpec((None, tm, D), lambda b, t: (b, t, 0)),
        out_shape=jax.ShapeDtypeStruct((B, n_rows, D), F32),
        compiler_params=_cparams(2),
        name="final_norm",
    )(h, w)


def _rope_tables(pos):
    half = ROT_DIM // 2
    inv_freq = ROPE_THETA ** (-jnp.arange(half, dtype=F32) / half)
    ang = pos.astype(F32)[:, None] * inv_freq[None, :]
    cos, sin = jnp.cos(ang), jnp.sin(ang)
    n = pos.shape[0]
    one = jnp.ones((n, HEAD_DIM - ROT_DIM), F32)
    zero = jnp.zeros((n, HEAD_DIM - ROT_DIM), F32)
    z8 = jnp.zeros((n, half), F32)
    c64 = jnp.concatenate([cos, cos, one], axis=1)
    a64 = jnp.concatenate([-sin, z8, zero], axis=1)
    b64 = jnp.concatenate([z8, sin, zero], axis=1)
    row_major = jnp.stack([jnp.tile(t, (1, LANE // HEAD_DIM)) for t in (c64, a64, b64)], axis=0)
    return row_major, jnp.stack([cos.T, sin.T], axis=0)


def _layer_weights(l, attn_norm, w_in, w_gate_up, b_gate, gla_norm, w_sconv, w_out, ffn_norm, w_up, w_ffn_conv, w_down):
    D = w_in.shape[1]
    wi = w_in[l]
    cuts = [0, 128, 256, 512, 528, 784, 1040, 1296, 1552, 2064, 2576, 3088, 3344, 3408, 3412]
    gq, gk, gv, glr, gr, sb, sc, sx, aq, ak, av, iq, ik, iw = [wi[:, cuts[i]:cuts[i + 1]] for i in range(14)]
    pad = jnp.zeros((D, LANE - GLA_RANK - IDX_HEADS), F32)
    wp = jnp.concatenate([gq, gk, gv, gr, sb, sc, sx, aq, iq, glr, iw, pad], axis=1).astype(BF16)
    wt = jnp.concatenate([ak, av, ik], axis=1).T.astype(BF16)
    wg = jnp.zeros((LANE, GLA_QK), F32).at[MISC_GLR:MISC_GLR + GLA_RANK].set(w_gate_up[l])
    d_ff = w_down.shape[1]
    n_ff = d_ff // FF_CHUNK
    wu = w_up[l].astype(BF16)
    wua = wu[:, :d_ff].reshape(D, n_ff, FF_CHUNK).transpose(1, 0, 2)
    wub = wu[:, d_ff:].reshape(D, n_ff, FF_CHUNK).transpose(1, 0, 2)
    wfc = w_ffn_conv[l].reshape(3, n_ff, FF_CHUNK).transpose(1, 0, 2)
    wdn = w_down[l].astype(BF16).reshape(n_ff, FF_CHUNK, D)
    return dict(nw=attn_norm[l][None], wp=wp, wt=wt, wg=wg, bg=b_gate[l][None], gn=gla_norm[l][None], wsc=w_sconv[l],
                wout=w_out[l].astype(BF16), fnw=ffn_norm[l][None], wua=wua, wub=wub, wfc=wfc, wdn=wdn)


def _unchunk(a):
    a = jnp.swapaxes(a, -3, -2)
    return a.reshape(a.shape[:-2] + (a.shape[-2] * a.shape[-1],))


def kernel(x_prompt, x_sample, cache_k, cache_v, cache_idx_k, page_table, state_gla, state_sconv, state_ffn_conv,
           meta_tokens, attn_norm, w_in, w_gate_up, b_gate, gla_norm, w_sconv, w_out, ffn_norm, w_up, w_ffn_conv,
           w_down, final_norm):
    B, S_len, D = x_prompt.shape
    depth = w_in.shape[0]
    T = S_len + N_META
    pad = (-T) % LANE
    R = pad + T
    DB, S = x_sample.shape[:2]
    n_pages = page_table.shape[1]
    past = n_pages * PAGE
    topk_p = min(TOPK_MAX, T // 4)
    topk_s = min(TOPK_MAX, (past + S) // 4)
    d_ff = w_down.shape[1]
    n_ff = d_ff // FF_CHUNK
    n_pool = cache_k.shape[1]
    assert S <= QPAD and (DB * S) % 8 == 0

    hp = jnp.concatenate([jnp.zeros((B, pad, D), F32), jnp.broadcast_to(meta_tokens[None], (B, N_META, D)), x_prompt],
                         axis=1)
    hs = x_sample.reshape(1, DB * S, D)
    rope_p, rope_pt = _rope_tables(jnp.maximum(jnp.arange(R) - pad, 0))
    rope_s, rope_st = _rope_tables(past + jnp.arange(DB * S) % S)
    ck = jnp.transpose(cache_k, (0, 1, 3, 4, 2)).reshape(depth, n_pool, ATT_W, PAGE)
    cv = jnp.transpose(cache_v, (0, 1, 3, 4, 2)).reshape(depth, n_pool, ATT_W, PAGE)
    cik = jnp.transpose(cache_idx_k, (0, 1, 3, 2))

    outs_p = [[] for _ in range(6)]
    outs_s = [[] for _ in range(6)]
    for l in range(depth):
        W = _layer_weights(l, attn_norm, w_in, w_gate_up, b_gate, gla_norm, w_sconv, w_out, ffn_norm, w_up,
                           w_ffn_conv, w_down)
        post_w = (W["wsc"], W["wout"], W["fnw"], W["wua"], W["wub"], W["wfc"], W["wdn"])

        proj_w = (W["nw"], W["wp"], W["wt"], W["wg"], W["bg"])
        gla, g, scv, q, iq, misc, kt, vt, ikt, ktb, vtb, iktb = _proj(hp, *proj_w, rope_p, rope_pt)
        o_gla, st_t = _gla_prompt(gla, g, W["gn"])
        o_att = _dsa_prompt(iq, misc, q, iktb, ktb, vtb, pad, topk_p)
        hp, st_s, st_f = _post(hp, o_gla, scv, o_att, *post_w)

        def heads_last(a):
            return a[:, :, pad:].reshape(B, ATT_HEADS, HEAD_DIM, T).transpose(0, 3, 1, 2)

        outs_p[0].append(heads_last(kt))
        outs_p[1].append(heads_last(vt))
        outs_p[2].append(ikt[:, :, pad:].transpose(0, 2, 1))
        outs_p[3].append(jnp.swapaxes(st_t, -1, -2))
        outs_p[4].append(st_s)
        outs_p[5].append(_unchunk(st_f))

        gla, g, scv, q, iq, misc, kt, vt, ikt, _, _, _ = _proj(hs, *proj_w, rope_s, rope_st)
        k, v, ik = kt[0].T, vt[0].T, ikt[0].T
        z = gla[0].reshape(DB, S, 768)
        tr = lambda a: jnp.swapaxes(a, 1, 2)
        eT = tr(g[0].reshape(DB, S, GLA_QK))
        qT = tr(z[:, :, 0:GLA_QK])
        kT = tr(z[:, :, GLA_QK:2 * GLA_QK])
        vx = jnp.repeat(z[:, :, 2 * GLA_QK:2 * GLA_QK + GLA_W].reshape(DB, S, GLA_HEADS, 1, GLA_DV), GLA_DK, axis=3)
        vx = vx.reshape(DB, S, GLA_QK, GLA_DV)
        gr = z[:, :, 2 * GLA_QK + GLA_W:]
        s0 = state_gla[l].reshape(DB, GLA_QK, GLA_DV)
        o_gla, s_new = _gla_sample(qT, kT, eT, vx, gr, W["gn"], s0)

        def hq(a, n_h, d):
            a = a.reshape(DB, S, n_h, d).transpose(0, 2, 1, 3)
            a = jnp.pad(a, ((0, 0), (0, 0), (0, QPAD - S), (0, 0)))
            return a.reshape(DB, n_h * QPAD, d)

        m0 = misc[0]
        iq_s = hq(iq[0], IDX_HEADS, IDX_DIM)
        w_s = hq(m0[:, MISC_IW:MISC_IW + IDX_HEADS], IDX_HEADS, 1)

        def new_page(a):
            a = a[0].reshape(a.shape[1], DB, S).transpose(1, 0, 2)
            return jnp.pad(a, ((0, 0), (0, 0), (0, PAGE - S)))

        keys = _idx_sample(page_table, iq_s, w_s, new_page(ikt), cik, l)
        thr, js = _thr_sample(keys[:, :S].reshape(DB * S, past + PAGE), topk_s)
        qpad = lambda a: jnp.pad(a.reshape(DB, S, LANE), ((0, 0), (0, QPAD - S), (0, 0)))
        thr, js = qpad(thr), qpad(js)
        q_s = jnp.pad(q[0].reshape(DB, S, ATT_W), ((0, 0), (0, QPAD - S), (0, 0)))
        o_att = _att_sample(page_table, q_s, keys, thr, js, new_page(kt), new_page(vt), ck, cv, l)[:, :S]

        def fix(prev, shift):
            p = jnp.zeros((DB, S) + prev.shape[2:], F32)
            for s in range(shift):
                p = p.at[:, s].set(prev[:, 2 - shift + s])
            return p.reshape((DB * S,) + prev.shape[2:])

        def chunked(a):
            return a.reshape(DB * S, n_ff, FF_CHUNK).transpose(1, 0, 2)

        sfix = (fix(state_sconv[l], 1), fix(state_sconv[l], 2),
                chunked(fix(state_ffn_conv[l], 1)), chunked(fix(state_ffn_conv[l], 2)))
        hs, u, a = _post(hs, o_gla.reshape(1, DB * S, GLA_W), scv, o_att.reshape(1, DB * S, ATT_W), *post_w,
                         sample_fix=sfix, n_tok=S)
        outs_s[0].append(k.reshape(DB, S, ATT_HEADS, HEAD_DIM))
        outs_s[1].append(v.reshape(DB, S, ATT_HEADS, HEAD_DIM))
        outs_s[2].append(ik.reshape(DB, S, IDX_DIM))
        outs_s[3].append(s_new.reshape(DB, GLA_HEADS, GLA_DK, GLA_DV))
        outs_s[4].append(u[0].reshape(DB, S, SCONV_W)[:, S - 2:])
        outs_s[5].append(_unchunk(a).reshape(DB, S, d_ff)[:, S - 2:])

    fw = final_norm[None]
    y_prompt = _final_norm(hp, fw, pad + N_META, S_len)
    y_sample = _final_norm(hs, fw, 0, DB * S).reshape(DB, S, D)
    sp = [jnp.stack(a, axis=0) for a in outs_p]
    ss = [jnp.stack(a, axis=0) for a in outs_s]
    return (y_prompt, y_sample, *sp, *ss)
```

```python
import functools

import jax
import jax.numpy as jnp
from jax import lax
from jax.experimental import pallas as pl
from jax.experimental.pallas import tpu as pltpu

F32 = jnp.float32
BF16 = jnp.bfloat16
I32 = jnp.int32

N_META = 16
GLA_HEADS, GLA_DK, GLA_DV = 4, 32, 64
GLA_RANK = 16
GLA_TAU = 16.0
GLA_W = GLA_HEADS * GLA_DV
GLA_QK = GLA_HEADS * GLA_DK
SCONV_W = 256
ATT_HEADS, HEAD_DIM = 8, 64
ATT_W = ATT_HEADS * HEAD_DIM
ROT_DIM = 16
ROPE_THETA = 500000.0
IDX_HEADS, IDX_DIM = 4, 64
IDX_W = IDX_HEADS * IDX_DIM
TOPK_MAX = 256
PAGE = 128
EPS = 1e-6
FF_CHUNK = 256

LANE = 128
QB = 128
GLA_CHUNK = 64
NEG_BIG = -1e30
VMEM_LIMIT = 56 * 1024 * 1024

SEG_GLA = (0, 768)
SEG_SCV = (768, 1536)
SEG_AQ = (1536, 2048)
SEG_IQ = (2048, 2304)
SEG_MISC = (2304, 2432)
N_PROJ = 2432
MISC_GLR = 0
MISC_IW = 16
TSEG_K = (0, 512)
TSEG_V = (512, 1024)
TSEG_IK = (1024, 1088)
N_PROJ_T = 1088

INT_MIN = -2147483648
NEG_INF_KEY = -2139095041


def _tile(n, cands=(512, 384, 256, 128, 64, 32, 16, 8)):
    for c in cands:
        if n % c == 0:
            return c
    raise ValueError(f"no tile for {n}")


def _dot(a, b):
    return jnp.dot(a, b, preferred_element_type=F32)


def _dot_nt(a, b):
    return lax.dot_general(a, b, (((1,), (1,)), ((), ())), preferred_element_type=F32)


def _dot_tn(a, b):
    return lax.dot_general(a, b, (((0,), (0,)), ((), ())), preferred_element_type=F32)


def _cparams(n_axes):
    return pltpu.CompilerParams(dimension_semantics=("arbitrary",) * n_axes, vmem_limit_bytes=VMEM_LIMIT)


def _const_spec(shape):
    nd = len(shape)
    return pl.BlockSpec(shape, lambda *_: (0,) * nd, pipeline_mode=pl.Buffered(1))


def _to_key(x):
    bits = pltpu.bitcast(x, I32)
    return bits ^ ((bits >> 31) & 0x7FFFFFFF)


def _proj_kernel(x_ref, nw_ref, w_ref, wt_ref, wg_ref, bg_ref, rope_ref, ropet_ref,
                 gla_ref, g_ref, scv_ref, q_ref, iq_ref, misc_ref,
                 kt_ref, vt_ref, ikt_ref, ktb_ref, vtb_ref, iktb_ref):
    x = x_ref[...]
    xn = x * lax.rsqrt(jnp.mean(x * x, axis=-1, keepdims=True) + EPS) * nw_ref[...]
    xb = xn.astype(BF16)

    def seg(s):
        return _dot(xb, w_ref[:, s[0]:s[1]])

    def seg_t(s):
        return _dot_nt(wt_ref[s[0]:s[1], :], xb)

    cos, s1, s2 = rope_ref[0], rope_ref[1], rope_ref[2]

    def rope128(z, c, a, b):
        return z * c + pltpu.roll(z, LANE - ROT_DIM // 2, 1) * a + pltpu.roll(z, ROT_DIM // 2, 1) * b

    gla_ref[...] = seg(SEG_GLA)
    scv_ref[...] = seg(SEG_SCV)

    zq = seg(SEG_AQ)
    for c in range(ATT_W // LANE):
        sl = slice(c * LANE, (c + 1) * LANE)
        q_ref[:, sl] = (rope128(zq[:, sl], cos, s1, s2) * HEAD_DIM ** -0.5).astype(BF16)
    zi = seg(SEG_IQ)
    for c in range(IDX_W // LANE):
        sl = slice(c * LANE, (c + 1) * LANE)
        iq_ref[:, sl] = (rope128(zi[:, sl], cos, s1, s2) * IDX_DIM ** -0.5).astype(BF16)

    zm = seg(SEG_MISC)
    gpre = jnp.dot(zm, wg_ref[...], preferred_element_type=F32, precision=lax.Precision.HIGHEST) + bg_ref[...]
    g_ref[...] = jax.nn.log_sigmoid(gpre) * (1.0 / GLA_TAU)
    lane = lax.broadcasted_iota(I32, zm.shape, 1)
    misc_ref[...] = zm * jnp.where((lane >= MISC_IW) & (lane < MISC_IW + IDX_HEADS), IDX_HEADS ** -0.5, 1.0)

    cos_t, sin_t = ropet_ref[0], ropet_ref[1]
    half = ROT_DIM // 2

    def store_roped_t(z, n_heads, ref32, ref16):
        for h in range(n_heads):
            r0 = h * HEAD_DIM
            x1, x2 = z[r0:r0 + half], z[r0 + half:r0 + 2 * half]
            parts = ((r0, x1 * cos_t - x2 * sin_t), (r0 + half, x2 * cos_t + x1 * sin_t),
                     (r0 + 2 * half, z[r0 + 2 * half:r0 + HEAD_DIM]))
            for start, val in parts:
                ref32[start:start + val.shape[0], :] = val
        ref16[...] = ref32[...].astype(BF16)

    store_roped_t(seg_t(TSEG_K), ATT_HEADS, kt_ref, ktb_ref)
    zv = seg_t(TSEG_V)
    vt_ref[...] = zv
    vtb_ref[...] = zv.astype(BF16)
    store_roped_t(seg_t(TSEG_IK), 1, ikt_ref, iktb_ref)


def _proj(x, nw, w, wt, wg, bg, rope, rope_t):
    B, R, D = x.shape
    tm = _tile(R)
    grid = (B, R // tm)

    def rspec(width):
        return pl.BlockSpec((None, tm, width), lambda b, t: (b, t, 0))

    def tspec(rows):
        return pl.BlockSpec((None, rows, tm), lambda b, t: (b, 0, t))

    outs = [(768, F32), (GLA_QK, F32), (768, F32), (ATT_W, BF16), (IDX_W, BF16), (LANE, F32)]
    outs_t = [(ATT_W, F32), (ATT_W, F32), (IDX_DIM, F32), (ATT_W, BF16), (ATT_W, BF16), (IDX_DIM, BF16)]
    return pl.pallas_call(
        _proj_kernel,
        grid=grid,
        in_specs=[rspec(D), _const_spec((1, D)), _const_spec((D, N_PROJ)), _const_spec((N_PROJ_T, D)),
                  _const_spec((LANE, GLA_QK)), _const_spec((1, GLA_QK)),
                  pl.BlockSpec((3, tm, LANE), lambda b, t: (0, t, 0)),
                  pl.BlockSpec((2, ROT_DIM // 2, tm), lambda b, t: (0, 0, t))],
        out_specs=[rspec(wd) for wd, _ in outs] + [tspec(r) for r, _ in outs_t],
        out_shape=[jax.ShapeDtypeStruct((B, R, wd), dt) for wd, dt in outs]
        + [jax.ShapeDtypeStruct((B, r, R), dt) for r, dt in outs_t],
        compiler_params=_cparams(2),
        name="proj",
    )(x, nw, w, wt, wg, bg, rope, rope_t)


def _gla_kernel(gla_ref, g_ref, gn_ref, o_ref, st_ref, s_sc):
    t = pl.program_id(1)
    C = GLA_CHUNK

    @pl.when(t == 0)
    def _():
        s_sc[...] = jnp.zeros_like(s_sc)

    r_i = lax.broadcasted_iota(I32, (C, C), 0)
    c_i = lax.broadcasted_iota(I32, (C, C), 1)
    causal = c_i <= r_i
    tri = jnp.where(causal, 1.0, 0.0).astype(F32)
    gn = gn_ref[...]
    for bi in range(gla_ref.shape[0]):
        z = gla_ref[bi]
        b = jnp.dot(tri, g_ref[bi], preferred_element_type=F32, precision=lax.Precision.HIGHEST)
        b_last = b[C - 1:C, :]
        b_mid = b[C // 2:C // 2 + 1, :]
        e_in = jnp.exp(b)
        e_q = jnp.exp(b - b_mid)
        e_k = jnp.exp(b_mid - b)
        e_out = jnp.exp(b_last - b)
        e_last = jnp.exp(b_last)
        qz = z[:, 0:GLA_QK] * GLA_DK ** -0.5
        kz = z[:, GLA_QK:2 * GLA_QK]
        q_in = (qz * e_in).astype(BF16)
        q_x = (qz * e_q).astype(BF16)
        k_x = (kz * e_k).astype(BF16)
        k_out = (kz * e_out).astype(BF16)
        for h in range(GLA_HEADS):
            ks = slice(h * GLA_DK, (h + 1) * GLA_DK)
            v = z[:, 2 * GLA_QK + h * GLA_DV:2 * GLA_QK + (h + 1) * GLA_DV]
            gr = z[:, 2 * GLA_QK + GLA_W + h * GLA_DV:2 * GLA_QK + GLA_W + (h + 1) * GLA_DV]
            vb = v.astype(BF16)
            st = s_sc[bi, h]
            att = jnp.where(causal, _dot_nt(q_x[:, ks], k_x[:, ks]), 0.0)
            o = _dot(att.astype(BF16), vb) + _dot_nt(q_in[:, ks], st.astype(BF16))
            s_sc[bi, h] = st * e_last[:, ks] + _dot_tn(vb, k_out[:, ks])
            on = o * lax.rsqrt(jnp.mean(o * o, axis=-1, keepdims=True) + EPS) * gn
            o_ref[bi, :, h * GLA_DV:(h + 1) * GLA_DV] = on * jax.nn.silu(gr)
    st_ref[...] = s_sc[...]


def _gla_prompt(gla, g, gn):
    B, R, _ = gla.shape
    C = GLA_CHUNK
    nb = _tile(B, (4, 2, 1))
    return pl.pallas_call(
        _gla_kernel,
        grid=(B // nb, R // C),
        in_specs=[pl.BlockSpec((nb, C, 768), lambda b, t: (b, t, 0)),
                  pl.BlockSpec((nb, C, GLA_QK), lambda b, t: (b, t, 0)),
                  _const_spec((1, GLA_DV))],
        out_specs=[pl.BlockSpec((nb, C, GLA_W), lambda b, t: (b, t, 0)),
                   pl.BlockSpec((nb, GLA_HEADS, GLA_DV, GLA_DK), lambda b, t: (b, 0, 0, 0))],
        out_shape=[jax.ShapeDtypeStruct((B, R, GLA_W), F32),
                   jax.ShapeDtypeStruct((B, GLA_HEADS, GLA_DV, GLA_DK), F32)],
        scratch_shapes=[pltpu.VMEM((nb, GLA_HEADS, GLA_DV, GLA_DK), F32)],
        compiler_params=_cparams(2),
        name="gla_prompt",
    )(gla, g, gn)


def _gla_step_kernel(qT_ref, kT_ref, gT_ref, vx_ref, gr_ref, gn_ref, s0_ref, o_ref, s_ref, *, n_tok):
    S = s0_ref[...]
    qT, kT, eT = qT_ref[...] * GLA_DK ** -0.5, kT_ref[...], jnp.exp(gT_ref[...])
    gn = gn_ref[...]
    gr = gr_ref[...]
    for t in range(n_tok):
        S = eT[:, t:t + 1] * S + kT[:, t:t + 1] * vx_ref[t]
        qs = qT[:, t:t + 1] * S
        for h in range(GLA_HEADS):
            o = jnp.sum(qs[h * GLA_DK:(h + 1) * GLA_DK, :], axis=0, keepdims=True)
            on = o * lax.rsqrt(jnp.mean(o * o, axis=-1, keepdims=True) + EPS) * gn
            hs = slice(h * GLA_DV, (h + 1) * GLA_DV)
            o_ref[t:t + 1, hs] = on * jax.nn.silu(gr[t:t + 1, hs])
    s_ref[...] = S


def _gla_sample(qT, kT, eT, vx, gr, gn, s0):
    DB, HK, n_tok = qT.shape

    def bspec(*shape):
        nd = len(shape)
        return pl.BlockSpec((None,) + shape, lambda b: (b,) + (0,) * nd)

    return pl.pallas_call(
        functools.partial(_gla_step_kernel, n_tok=n_tok),
        grid=(DB,),
        in_specs=[bspec(HK, n_tok), bspec(HK, n_tok), bspec(HK, n_tok), bspec(n_tok, HK, GLA_DV),
                  bspec(n_tok, GLA_W), _const_spec((1, GLA_DV)), bspec(HK, GLA_DV)],
        out_specs=[bspec(n_tok, GLA_W), bspec(HK, GLA_DV)],
        out_shape=[jax.ShapeDtypeStruct((DB, n_tok, GLA_W), F32), jax.ShapeDtypeStruct((DB, HK, GLA_DV), F32)],
        compiler_params=_cparams(1),
        name="gla_sample",
    )(qT, kT, eT, vx, gr, gn, s0)


def _chunk_loop(n_chunks, body, init):
    return body(0, init) if isinstance(n_chunks, int) and n_chunks == 1 else lax.fori_loop(0, n_chunks, body, init)


def _count_cols(key_ref, rsl, rows, n_chunks, cw, pred):
    def body(c, acc):
        off = pl.multiple_of(c * cw, cw)
        col = off + lax.broadcasted_iota(I32, (rows, LANE), 1)
        for j in range(cw // LANE):
            k = key_ref[rsl, pl.ds(off + j * LANE, LANE)]
            acc = acc + jnp.where(pred(k, col + j * LANE), 1, 0)
        return acc

    return _chunk_loop(n_chunks, body, jnp.zeros((rows, LANE), I32))


def _selected(key, col, thr, js):
    return (key - jnp.where(col >= js, 1, 0)) >= thr


def _select_threshold(key_ref, thr_ref, js_ref, n_groups, rows, n_chunks, cw, topk, idx_bits):
    gs = range(n_groups)
    rsl = [slice(g * rows, (g + 1) * rows) for g in gs]

    def partial_count(g, pred):
        return _count_cols(key_ref, rsl[g], rows, n_chunks, cw, pred)

    def count(g, pred):
        return jnp.sum(partial_count(g, pred), axis=1, keepdims=True)

    def start(g):
        return jnp.where(count(g, lambda k, col: k >= 0) >= topk, 0, INT_MIN).astype(I32)

    def bs(it, los):
        bit = lax.shift_left(jnp.int32(1), 30 - it)
        cands = [los[g] + bit for g in gs]
        parts = [partial_count(g, lambda k, col, cand=cands[g]: k >= cand) for g in gs]
        return tuple(jnp.where(jnp.sum(parts[g], axis=1, keepdims=True) >= topk, cands[g], los[g]) for g in gs)

    thrs = lax.fori_loop(0, 31, bs, tuple(start(g) for g in gs))

    for g in gs:
        thr = thrs[g]
        n_gt = count(g, lambda k, col: k > thr)
        n_ge = count(g, lambda k, col: k >= thr)
        need = topk - n_gt
        excess = (n_ge > topk) & (thr > NEG_INF_KEY)
        base = jnp.where(thr > NEG_INF_KEY, 1 << idx_bits, 0).astype(I32)
        thr_ref[rsl[g], :] = jnp.broadcast_to(thr, (rows, LANE))
        js_ref[rsl[g], :] = jnp.broadcast_to(base, (rows, LANE))

        @pl.when(jnp.max(jnp.where(excess, 1, 0)) > 0)
        def _(g=g, thr=thr, need=need, excess=excess, base=base):
            def bs2(it, j):
                cand = j + lax.shift_left(jnp.int32(1), idx_bits - 1 - it)
                c = count(g, lambda k, col: (k == thr) & (col < cand))
                return jnp.where(c < need, cand, j)

            j = lax.fori_loop(0, idx_bits, bs2, jnp.zeros((rows, 1), I32))
            js_ref[rsl[g], :] = jnp.broadcast_to(jnp.where(excess, j + 1, base), (rows, LANE))


def _dsa_prompt_kernel(iq_ref, misc_ref, q_ref, ikt_ref, kt_ref, vt_ref, o_ref,
                       key_sc, thr_sc, js_sc, mp_sc, lp_sc, acc_sc, *, pad, topk, qb, idx_bits):
    i = pl.program_id(1)
    kc = qb
    n_kc = i + 1
    n_slab = kc // LANE
    row = i * qb + lax.broadcasted_iota(I32, (qb, 1), 0)

    def cols(c):
        return pl.ds(pl.multiple_of(c * kc, kc), kc)

    misc = misc_ref[...]
    iq = iq_ref[...]
    iqh = [iq[:, h * IDX_DIM:(h + 1) * IDX_DIM] for h in range(IDX_HEADS)]
    ws = [misc[:, MISC_IW + h:MISC_IW + h + 1] for h in range(IDX_HEADS)]

    def p1(c, carry):
        ikt = ikt_ref[:, cols(c)]
        sc = ws[0] * jnp.maximum(_dot(iqh[0], ikt), 0.0)
        for h in range(1, IDX_HEADS):
            sc = sc + ws[h] * jnp.maximum(_dot(iqh[h], ikt), 0.0)
        col = c * kc + lax.broadcasted_iota(I32, (qb, kc), 1)
        sc = jnp.where((col <= row) & (col >= pad), sc + 0.0, -jnp.inf)
        key_sc[:, cols(c)] = _to_key(sc)
        return carry

    lax.fori_loop(0, n_kc, p1, 0)

    n_groups = 4
    _select_threshold(key_sc, thr_sc, js_sc, n_groups, qb // n_groups, n_kc, kc, topk, idx_bits)
    thr = thr_sc[:, 0:1]
    js = js_sc[:, 0:1]

    def pmask(c, carry):
        col = c * kc + lax.broadcasted_iota(I32, (qb, kc), 1)
        sel = _selected(key_sc[:, cols(c)], col, thr, js)
        key_sc[:, cols(c)] = pltpu.bitcast(jnp.where(sel, 0.0, NEG_BIG), I32)
        return carry

    lax.fori_loop(0, n_kc, pmask, 0)

    q = q_ref[...]
    qh = [q[:, h * HEAD_DIM:(h + 1) * HEAD_DIM] for h in range(ATT_HEADS)]

    def scores(c, h):
        hs = slice(h * HEAD_DIM, (h + 1) * HEAD_DIM)
        return _dot(qh[h], kt_ref[hs, cols(c)]) + pltpu.bitcast(key_sc[:, cols(c)], F32)

    mp_sc[...] = jnp.full(mp_sc.shape, NEG_BIG, F32)

    def pa(c, carry):
        for h in range(ATT_HEADS):
            s = scores(c, h)
            mp = mp_sc[h]
            for j in range(n_slab):
                mp = jnp.maximum(mp, s[:, j * LANE:(j + 1) * LANE])
            mp_sc[h] = mp
        return carry

    lax.fori_loop(0, n_kc, pa, 0)
    for h in range(ATT_HEADS):
        mp_sc[h] = jnp.broadcast_to(jnp.max(mp_sc[h], axis=1, keepdims=True), (qb, LANE))

    lp_sc[...] = jnp.zeros_like(lp_sc)
    acc_sc[...] = jnp.zeros_like(acc_sc)

    def pb(c, carry):
        for h in range(ATT_HEADS):
            hs = slice(h * HEAD_DIM, (h + 1) * HEAD_DIM)
            s = scores(c, h)
            m = mp_sc[h]
            ps = [jnp.exp(s[:, j * LANE:(j + 1) * LANE] - m) for j in range(n_slab)]
            lp = lp_sc[h]
            for pj in ps:
                lp = lp + pj
            lp_sc[h] = lp
            p = jnp.concatenate(ps, axis=1).astype(BF16)
            acc_sc[h] += _dot_nt(p, vt_ref[hs, cols(c)])
        return carry

    lax.fori_loop(0, n_kc, pb, 0)

    valid = row >= pad
    for h in range(ATT_HEADS):
        l = jnp.sum(lp_sc[h], axis=1, keepdims=True)
        o_ref[:, h * HEAD_DIM:(h + 1) * HEAD_DIM] = jnp.where(valid, acc_sc[h] / l, 0.0)


def _dsa_prompt(iq, misc, q, iktb, ktb, vtb, pad, topk):
    B, R, _ = q.shape
    qb = _tile(R, (384, 256, 128))
    assert qb >= topk
    idx_bits = max(1, (R - 1).bit_length())

    def qspec(width):
        return pl.BlockSpec((None, qb, width), lambda b, i: (b, i, 0))

    def fspec(rows):
        return pl.BlockSpec((None, rows, R), lambda b, i: (b, 0, 0))

    return pl.pallas_call(
        functools.partial(_dsa_prompt_kernel, pad=pad, topk=topk, qb=qb, idx_bits=idx_bits),
        grid=(B, R // qb),
        in_specs=[qspec(IDX_W), qspec(LANE), qspec(ATT_W), fspec(IDX_DIM), fspec(ATT_W), fspec(ATT_W)],
        out_specs=qspec(ATT_W),
        out_shape=jax.ShapeDtypeStruct((B, R, ATT_W), F32),
        scratch_shapes=[pltpu.VMEM((qb, R), I32), pltpu.VMEM((qb, LANE), I32), pltpu.VMEM((qb, LANE), I32),
                        pltpu.VMEM((ATT_HEADS, qb, LANE), F32), pltpu.VMEM((ATT_HEADS, qb, LANE), F32),
                        pltpu.VMEM((ATT_HEADS, qb, HEAD_DIM), F32)],
        compiler_params=_cparams(2),
        name="dsa_prompt",
    )(iq, misc, q, iktb, ktb, vtb)


QPAD = 8


def _idx_sample_kernel(pt_ref, iq_ref, w_ref, iknt_ref, *rest, pg, n_pg, past):
    pages = rest[:pg]
    key_ref = rest[pg]
    g = pl.program_id(1)
    iq = iq_ref[...]
    w = w_ref[...]

    def score(ikt):
        r = jnp.maximum(_dot(iq, ikt), 0.0) * w
        s = r[0:QPAD]
        for h in range(1, IDX_HEADS):
            s = s + r[h * QPAD:(h + 1) * QPAD]
        return s + 0.0

    ikt = jnp.concatenate([p[...].astype(BF16) for p in pages], axis=1)
    off = pl.multiple_of(g * (pg * PAGE), pg * PAGE)
    key_ref[:, pl.ds(off, pg * PAGE)] = _to_key(score(ikt))

    @pl.when(g == n_pg - 1)
    def _():
        sn = score(iknt_ref[...].astype(BF16))
        qrow = lax.broadcasted_iota(I32, (QPAD, PAGE), 0)
        colr = lax.broadcasted_iota(I32, (QPAD, PAGE), 1)
        key_ref[:, past:past + PAGE] = _to_key(jnp.where(colr <= qrow, sn, -jnp.inf))


def _idx_sample(page_table, iq, w, iknt, cache_ikt, layer):
    DB, n_pages = page_table.shape
    past = n_pages * PAGE
    pg = 16 if n_pages % 16 == 0 else (4 if n_pages % 4 == 0 else 1)
    n_pg = n_pages // pg
    ncol = past + PAGE
    rows = IDX_HEADS * QPAD

    def page_spec(j):
        return pl.BlockSpec((None, None, IDX_DIM, PAGE), lambda b, g, pt: (layer, pt[b, g * pg + j], 0, 0))

    def bspec(*shape):
        nd = len(shape)
        return pl.BlockSpec((None,) + shape, lambda b, g, pt: (b,) + (0,) * nd)

    grid_spec = pltpu.PrefetchScalarGridSpec(
        num_scalar_prefetch=1,
        grid=(DB, n_pg),
        in_specs=[bspec(rows, IDX_DIM), bspec(rows, 1), bspec(IDX_DIM, PAGE)] + [page_spec(j) for j in range(pg)],
        out_specs=bspec(QPAD, ncol),
    )
    return pl.pallas_call(
        functools.partial(_idx_sample_kernel, pg=pg, n_pg=n_pg, past=past),
        grid_spec=grid_spec,
        out_shape=jax.ShapeDtypeStruct((DB, QPAD, ncol), I32),
        compiler_params=_cparams(2),
        name="idx_sample",
    )(page_table, iq, w, iknt, *([cache_ikt] * pg))


def _thr_sample_kernel(key_ref, thr_ref, js_ref, *, n_groups, n_chunks, cw, topk, idx_bits):
    _select_threshold(key_ref, thr_ref, js_ref, n_groups, key_ref.shape[0] // n_groups, n_chunks, cw, topk, idx_bits)


def _thr_sample(keys, topk):
    n_rows, ncol = keys.shape
    rb = _tile(n_rows, (64, 32, 16))
    n_groups = 2 if rb % 32 == 0 else 1
    n_slab = ncol // LANE
    per_chunk = max(d for d in range(1, 65) if n_slab % d == 0)
    return pl.pallas_call(
        functools.partial(_thr_sample_kernel, n_groups=n_groups, n_chunks=n_slab // per_chunk, cw=per_chunk * LANE,
                          topk=topk, idx_bits=max(1, (ncol - 1).bit_length())),
        grid=(n_rows // rb,),
        in_specs=[pl.BlockSpec((rb, ncol), lambda i: (i, 0))],
        out_specs=[pl.BlockSpec((rb, LANE), lambda i: (i, 0)), pl.BlockSpec((rb, LANE), lambda i: (i, 0))],
        out_shape=[jax.ShapeDtypeStruct((n_rows, LANE), I32), jax.ShapeDtypeStruct((n_rows, LANE), I32)],
        compiler_params=_cparams(1),
        name="thr_sample",
    )(keys)


def _att_sample_kernel(pt_ref, q_ref, keyp_ref, keyn_ref, thr_ref, js_ref, kn_ref, vn_ref, *rest, pg, n_pg, past):
    kpages = rest[:pg]
    vpages = rest[pg:2 * pg]
    o_ref = rest[2 * pg]
    m_sc, l_sc, acc_sc = rest[2 * pg + 1:]
    g = pl.program_id(1)
    rows = ATT_HEADS * QPAD

    @pl.when(g == 0)
    def _():
        m_sc[...] = jnp.full(m_sc.shape, NEG_BIG, F32)
        l_sc[...] = jnp.zeros_like(l_sc)
        acc_sc[...] = jnp.zeros_like(acc_sc)

    q = q_ref[...]
    r_h = lax.broadcasted_iota(I32, (rows, ATT_W), 0) >> 3
    l_h = lax.broadcasted_iota(I32, (rows, ATT_W), 1) >> 6
    own = r_h == l_h
    qbd = jnp.where(own, jnp.concatenate([q.astype(F32)] * ATT_HEADS, axis=0), 0.0).astype(BF16)
    thr = thr_ref[:, 0:1]
    js = js_ref[:, 0:1]

    def update(kts, vts, key, colbase):
        width = key.shape[1]
        col = colbase + lax.broadcasted_iota(I32, (QPAD, width), 1)
        bias8 = jnp.where(_selected(key, col, thr, js), 0.0, NEG_BIG)
        s = jnp.concatenate([_dot(qbd, kt.astype(BF16)) for kt in kts], axis=1)
        s = s + jnp.concatenate([bias8] * ATT_HEADS, axis=0)
        m_old = m_sc[...]
        m_new = jnp.maximum(m_old, jnp.max(s, axis=1, keepdims=True))
        alpha = jnp.exp(m_old - m_new)
        pe = jnp.exp(s - m_new[:, 0:1])
        l_sc[...] = alpha * l_sc[...] + jnp.sum(pe, axis=1, keepdims=True)
        m_sc[...] = m_new
        pv = _dot_nt(pe[:, 0:PAGE].astype(BF16), vts[0].astype(BF16))
        for j in range(1, len(vts)):
            pv = pv + _dot_nt(pe[:, j * PAGE:(j + 1) * PAGE].astype(BF16), vts[j].astype(BF16))
        acc_sc[...] = alpha[:, 0:1] * acc_sc[...] + pv

    update([p[...] for p in kpages], [p[...] for p in vpages], keyp_ref[...], g * (pg * PAGE))

    @pl.when(g == n_pg - 1)
    def _():
        update([kn_ref[...]], [vn_ref[...]], keyn_ref[...], past)
        res = jnp.where(own, acc_sc[...] / l_sc[:, 0:1], 0.0)
        o = res[0:QPAD]
        for h in range(1, ATT_HEADS):
            o = o + res[h * QPAD:(h + 1) * QPAD]
        o_ref[...] = o


def _att_sample(page_table, q, keys, thr, js, kn, vn, cache_k, cache_v, layer):
    DB, n_pages = page_table.shape
    past = n_pages * PAGE
    pg = 16 if n_pages % 16 == 0 else (4 if n_pages % 4 == 0 else 1)
    n_pg = n_pages // pg
    rows = ATT_HEADS * QPAD

    def kv_spec(j):
        return pl.BlockSpec((None, None, ATT_W, PAGE), lambda b, g, pt: (layer, pt[b, g * pg + j], 0, 0))

    def bspec(*shape):
        nd = len(shape)
        return pl.BlockSpec((None,) + shape, lambda b, g, pt: (b,) + (0,) * nd)

    grid_spec = pltpu.PrefetchScalarGridSpec(
        num_scalar_prefetch=1,
        grid=(DB, n_pg),
        in_specs=[bspec(QPAD, ATT_W),
                  pl.BlockSpec((None, QPAD, pg * PAGE), lambda b, g, pt: (b, 0, g)),
                  pl.BlockSpec((None, QPAD, PAGE), lambda b, g, pt: (b, 0, n_pages)),
                  bspec(QPAD, LANE), bspec(QPAD, LANE), bspec(ATT_W, PAGE), bspec(ATT_W, PAGE)]
        + [kv_spec(j) for j in range(pg)] + [kv_spec(j) for j in range(pg)],
        out_specs=bspec(QPAD, ATT_W),
        scratch_shapes=[pltpu.VMEM((rows, PAGE), F32), pltpu.VMEM((rows, PAGE), F32), pltpu.VMEM((rows, ATT_W), F32)],
    )
    return pl.pallas_call(
        functools.partial(_att_sample_kernel, pg=pg, n_pg=n_pg, past=past),
        grid_spec=grid_spec,
        out_shape=jax.ShapeDtypeStruct((DB, QPAD, ATT_W), F32),
        compiler_params=_cparams(2),
        name="att_sample",
    )(page_table, q, keys, keys, thr, js, kn, vn, *([cache_k] * pg), *([cache_v] * pg))


def _conv3(u, w, first, second):
    u1 = jnp.where(first[0], first[1], pltpu.roll(u, 1, 0))
    u2 = jnp.where(second[0], second[1], pltpu.roll(u, 2, 0))
    return w[0:1] * u2 + w[1:2] * u1 + w[2:3] * u


def _post_kernel(*refs, sample, n_tok, n_ff):
    if sample:
        (h_ref, og_ref, scv_ref, oa_ref, p1s_ref, p2s_ref, p1f_ref, p2f_ref, wsc_ref, wout_ref, fnw_ref, wua_ref,
         wub_ref, wfc_ref, wdn_ref, h2_ref, u_ref, a_ref, acc_sc, hn_sc) = refs
    else:
        (h_ref, og_ref, scv_ref, oa_ref, wsc_ref, wout_ref, fnw_ref, wua_ref, wub_ref, wfc_ref, wdn_ref,
         h2_ref, sts_ref, stf_ref, cs_sc, cf_sc, acc_sc, hn_sc) = refs
    tm = h_ref.shape[0]

    if not sample:
        @pl.when(pl.program_id(1) == 0)
        def _():
            cs_sc[...] = jnp.zeros_like(cs_sc)
            cf_sc[...] = jnp.zeros_like(cf_sc)

    def fixes(width, c0, c1, p1, p2):
        rows = lax.broadcasted_iota(I32, (tm, width), 0)
        if sample:
            sp = lax.rem(rows, n_tok)
            return (sp < 1, p1), (sp < 2, p2)
        return (rows == 0, c1), (rows < 2, jnp.where(rows == 0, c0, c1))

    scv = scv_ref[...]
    sb = scv[:, 0:SCONV_W]
    u = scv[:, SCONV_W:2 * SCONV_W] * scv[:, 2 * SCONV_W:3 * SCONV_W]
    if sample:
        f1, f2 = fixes(SCONV_W, None, None, p1s_ref[...], p2s_ref[...])
        u_ref[...] = u
    else:
        f1, f2 = fixes(SCONV_W, cs_sc[0:1, :], cs_sc[1:2, :], None, None)
        cs_sc[0:2, :] = u[tm - 2:tm, :]
        sts_ref[...] = u[tm - 2:tm, :]
    o_s = sb * _conv3(u, wsc_ref[...], f1, f2)

    mix = (_dot(og_ref[...].astype(BF16), wout_ref[0:GLA_W, :])
           + _dot(o_s.astype(BF16), wout_ref[GLA_W:GLA_W + SCONV_W, :])
           + _dot(oa_ref[...].astype(BF16), wout_ref[GLA_W + SCONV_W:, :]))
    h1 = h_ref[...] + mix
    hn_sc[...] = (h1 * lax.rsqrt(jnp.mean(h1 * h1, axis=-1, keepdims=True) + EPS) * fnw_ref[...]).astype(BF16)
    acc_sc[...] = h1

    def ff(c, carry):
        hn = hn_sc[...]
        a = _dot(hn, wua_ref[c])
        bgate = _dot(hn, wub_ref[c])
        if sample:
            f1, f2 = fixes(FF_CHUNK, None, None, p1f_ref[c], p2f_ref[c])
            a_ref[c] = a
        else:
            f1, f2 = fixes(FF_CHUNK, cf_sc[c, 0:1, :], cf_sc[c, 1:2, :], None, None)
            cf_sc[c, 0:2, :] = a[tm - 2:tm, :]
            stf_ref[c] = a[tm - 2:tm, :]
        ac = _conv3(a, wfc_ref[c], f1, f2)
        act = (jax.nn.silu(ac) * bgate).astype(BF16)
        acc_sc[...] += _dot(act, wdn_ref[c])
        return carry

    lax.fori_loop(0, n_ff, ff, 0)
    h2_ref[...] = acc_sc[...]


def _post(h, og, scv, oa, wsc, wout, fnw, wua, wub, wfc, wdn, sample_fix=None, n_tok=1):
    B, R, D = h.shape
    n_ff = wua.shape[0]
    sample = sample_fix is not None
    tm = _tile(R, (704, 512, 384, 256, 128, 64, 32, 16, 8))

    def rspec(width):
        return pl.BlockSpec((None, tm, width), lambda b, t: (b, t, 0))

    in_specs = [rspec(D), rspec(GLA_W), rspec(768), rspec(ATT_W)]
    args = [h, og, scv, oa]
    if sample:
        assert B == 1 and R == tm
        p1s, p2s, p1f, p2f = sample_fix
        in_specs += [_const_spec((tm, SCONV_W)), _const_spec((tm, SCONV_W)),
                     _const_spec((n_ff, tm, FF_CHUNK)), _const_spec((n_ff, tm, FF_CHUNK))]
        args += [p1s, p2s, p1f, p2f]
        out_specs = [rspec(D), rspec(SCONV_W), pl.BlockSpec((n_ff, tm, FF_CHUNK), lambda b, t: (0, 0, 0))]
        out_shape = [jax.ShapeDtypeStruct((B, R, D), F32), jax.ShapeDtypeStruct((B, R, SCONV_W), F32),
                     jax.ShapeDtypeStruct((n_ff, R, FF_CHUNK), F32)]
        scratch = []
    else:
        out_specs = [rspec(D), pl.BlockSpec((None, 2, SCONV_W), lambda b, t: (b, 0, 0)),
                     pl.BlockSpec((None, n_ff, 2, FF_CHUNK), lambda b, t: (b, 0, 0, 0))]
        out_shape = [jax.ShapeDtypeStruct((B, R, D), F32), jax.ShapeDtypeStruct((B, 2, SCONV_W), F32),
                     jax.ShapeDtypeStruct((B, n_ff, 2, FF_CHUNK), F32)]
        scratch = [pltpu.VMEM((8, SCONV_W), F32), pltpu.VMEM((n_ff, 8, FF_CHUNK), F32)]
    in_specs += [_const_spec((3, SCONV_W)), _const_spec((D, D)), _const_spec((1, D)),
                 _const_spec((n_ff, D, FF_CHUNK)), _const_spec((n_ff, D, FF_CHUNK)),
                 _const_spec((n_ff, 3, FF_CHUNK)), _const_spec((n_ff, FF_CHUNK, D))]
    args += [wsc, wout, fnw, wua, wub, wfc, wdn]
    scratch += [pltpu.VMEM((tm, D), F32), pltpu.VMEM((tm, D), BF16)]
    return pl.pallas_call(
        functools.partial(_post_kernel, sample=sample, n_tok=n_tok, n_ff=n_ff),
        grid=(B, R // tm),
        in_specs=in_specs,
        out_specs=out_specs,
        out_shape=out_shape,
        scratch_shapes=scratch,
        compiler_params=_cparams(2),
        name="post_sample" if sample else "post_prompt",
    )(*args)


def _norm_kernel(x_ref, w_ref, o_ref):
    x = x_ref[...]
    o_ref[...] = x * lax.rsqrt(jnp.mean(x * x, axis=-1, keepdims=True) + EPS) * w_ref[...]


def _final_norm(h, w, skip_rows, n_rows):
    B, _, D = h.shape
    tm = LANE if n_rows % LANE == 0 else n_rows
    assert skip_rows % tm == 0 and n_rows % tm == 0
    off = skip_rows // tm
    return pl.pallas_call(
        _norm_kernel,
        grid=(B, n_rows // tm),
        in_specs=[pl.BlockSpec((None, tm, D), lambda b, t: (b, t + off, 0)), _const_spec((1, D))],
        out_specs=pl.BlockSpec((None, tm, D), lambda b, t: (b, t, 0)),
        out_shape=jax.ShapeDtypeStruct((B, n_rows, D), F32),
        compiler_params=_cparams(2),
        name="final_norm",
    )(h, w)


def _rope_tables(pos):
    half = ROT_DIM // 2
    inv_freq = ROPE_THETA ** (-jnp.arange(half, dtype=F32) / half)
    ang = pos.astype(F32)[:, None] * inv_freq[None, :]
    cos, sin = jnp.cos(ang), jnp.sin(ang)
    n = pos.shape[0]
    one = jnp.ones((n, HEAD_DIM - ROT_DIM), F32)
    zero = jnp.zeros((n, HEAD_DIM - ROT_DIM), F32)
    z8 = jnp.zeros((n, half), F32)
    c64 = jnp.concatenate([cos, cos, one], axis=1)
    a64 = jnp.concatenate([-sin, z8, zero], axis=1)
    b64 = jnp.concatenate([z8, sin, zero], axis=1)
    row_major = jnp.stack([jnp.tile(t, (1, LANE // HEAD_DIM)) for t in (c64, a64, b64)], axis=0)
    return row_major, jnp.stack([cos.T, sin.T], axis=0)


def _layer_weights(l, attn_norm, w_in, w_gate_up, b_gate, gla_norm, w_sconv, w_out, ffn_norm, w_up, w_ffn_conv, w_down):
    D = w_in.shape[1]
    wi = w_in[l]
    cuts = [0, 128, 256, 512, 528, 784, 1040, 1296, 1552, 2064, 2576, 3088, 3344, 3408, 3412]
    gq, gk, gv, glr, gr, sb, sc, sx, aq, ak, av, iq, ik, iw = [wi[:, cuts[i]:cuts[i + 1]] for i in range(14)]
    pad = jnp.zeros((D, LANE - GLA_RANK - IDX_HEADS), F32)
    wp = jnp.concatenate([gq, gk, gv, gr, sb, sc, sx, aq, iq, glr, iw, pad], axis=1).astype(BF16)
    wt = jnp.concatenate([ak, av, ik], axis=1).T.astype(BF16)
    wg = jnp.zeros((LANE, GLA_QK), F32).at[MISC_GLR:MISC_GLR + GLA_RANK].set(w_gate_up[l])
    d_ff = w_down.shape[1]
    n_ff = d_ff // FF_CHUNK
    wu = w_up[l].astype(BF16)
    wua = wu[:, :d_ff].reshape(D, n_ff, FF_CHUNK).transpose(1, 0, 2)
    wub = wu[:, d_ff:].reshape(D, n_ff, FF_CHUNK).transpose(1, 0, 2)
    wfc = w_ffn_conv[l].reshape(3, n_ff, FF_CHUNK).transpose(1, 0, 2)
    wdn = w_down[l].astype(BF16).reshape(n_ff, FF_CHUNK, D)
    return dict(nw=attn_norm[l][None], wp=wp, wt=wt, wg=wg, bg=b_gate[l][None], gn=gla_norm[l][None], wsc=w_sconv[l],
                wout=w_out[l].astype(BF16), fnw=ffn_norm[l][None], wua=wua, wub=wub, wfc=wfc, wdn=wdn)


def _unchunk(a):
    a = jnp.swapaxes(a, -3, -2)
    return a.reshape(a.shape[:-2] + (a.shape[-2] * a.shape[-1],))


def kernel(x_prompt, x_sample, cache_k, cache_v, cache_idx_k, page_table, state_gla, state_sconv, state_ffn_conv,
           meta_tokens, attn_norm, w_in, w_gate_up, b_gate, gla_norm, w_sconv, w_out, ffn_norm, w_up, w_ffn_conv,
           w_down, final_norm):
    B, S_len, D = x_prompt.shape
    depth = w_in.shape[0]
    T = S_len + N_META
    pad = (-T) % LANE
    R = pad + T
    DB, S = x_sample.shape[:2]
    n_pages = page_table.shape[1]
    past = n_pages * PAGE
    topk_p = min(TOPK_MAX, T // 4)
    topk_s = min(TOPK_MAX, (past + S) // 4)
    d_ff = w_down.shape[1]
    n_ff = d_ff // FF_CHUNK
    n_pool = cache_k.shape[1]
    assert S <= QPAD and (DB * S) % 8 == 0

    hp = jnp.concatenate([jnp.zeros((B, pad, D), F32), jnp.broadcast_to(meta_tokens[None], (B, N_META, D)), x_prompt],
                         axis=1)
    hs = x_sample.reshape(1, DB * S, D)
    rope_p, rope_pt = _rope_tables(jnp.maximum(jnp.arange(R) - pad, 0))
    rope_s, rope_st = _rope_tables(past + jnp.arange(DB * S) % S)
    ck = jnp.transpose(cache_k, (0, 1, 3, 4, 2)).reshape(depth, n_pool, ATT_W, PAGE)
    cv = jnp.transpose(cache_v, (0, 1, 3, 4, 2)).reshape(depth, n_pool, ATT_W, PAGE)
    cik = jnp.transpose(cache_idx_k, (0, 1, 3, 2))

    outs_p = [[] for _ in range(6)]
    outs_s = [[] for _ in range(6)]
    for l in range(depth):
        W = _layer_weights(l, attn_norm, w_in, w_gate_up, b_gate, gla_norm, w_sconv, w_out, ffn_norm, w_up,
                           w_ffn_conv, w_down)
        post_w = (W["wsc"], W["wout"], W["fnw"], W["wua"], W["wub"], W["wfc"], W["wdn"])

        proj_w = (W["nw"], W["wp"], W["wt"], W["wg"], W["bg"])
        gla, g, scv, q, iq, misc, kt, vt, ikt, ktb, vtb, iktb = _proj(hp, *proj_w, rope_p, rope_pt)
        o_gla, st_t = _gla_prompt(gla, g, W["gn"])
        o_att = _dsa_prompt(iq, misc, q, iktb, ktb, vtb, pad, topk_p)
        hp, st_s, st_f = _post(hp, o_gla, scv, o_att, *post_w)

        def heads_last(a):
            return a[:, :, pad:].reshape(B, ATT_HEADS, HEAD_DIM, T).transpose(0, 3, 1, 2)

        outs_p[0].append(heads_last(kt))
        outs_p[1].append(heads_last(vt))
        outs_p[2].append(ikt[:, :, pad:].transpose(0, 2, 1))
        outs_p[3].append(jnp.swapaxes(st_t, -1, -2))
        outs_p[4].append(st_s)
        outs_p[5].append(_unchunk(st_f))

        gla, g, scv, q, iq, misc, kt, vt, ikt, _, _, _ = _proj(hs, *proj_w, rope_s, rope_st)
        k, v, ik = kt[0].T, vt[0].T, ikt[0].T
        z = gla[0].reshape(DB, S, 768)
        tr = lambda a: jnp.swapaxes(a, 1, 2)
        eT = tr(g[0].reshape(DB, S, GLA_QK))
        qT = tr(z[:, :, 0:GLA_QK])
        kT = tr(z[:, :, GLA_QK:2 * GLA_QK])
        vx = jnp.repeat(z[:, :, 2 * GLA_QK:2 * GLA_QK + GLA_W].reshape(DB, S, GLA_HEADS, 1, GLA_DV), GLA_DK, axis=3)
        vx = vx.reshape(DB, S, GLA_QK, GLA_DV)
        gr = z[:, :, 2 * GLA_QK + GLA_W:]
        s0 = state_gla[l].reshape(DB, GLA_QK, GLA_DV)
        o_gla, s_new = _gla_sample(qT, kT, eT, vx, gr, W["gn"], s0)

        def hq(a, n_h, d):
            a = a.reshape(DB, S, n_h, d).transpose(0, 2, 1, 3)
            a = jnp.pad(a, ((0, 0), (0, 0), (0, QPAD - S), (0, 0)))
            return a.reshape(DB, n_h * QPAD, d)

        m0 = misc[0]
        iq_s = hq(iq[0], IDX_HEADS, IDX_DIM)
        w_s = hq(m0[:, MISC_IW:MISC_IW + IDX_HEADS], IDX_HEADS, 1)

        def new_page(a):
            a = a[0].reshape(a.shape[1], DB, S).transpose(1, 0, 2)
            return jnp.pad(a, ((0, 0), (0, 0), (0, PAGE - S)))

        keys = _idx_sample(page_table, iq_s, w_s, new_page(ikt), cik, l)
        thr, js = _thr_sample(keys[:, :S].reshape(DB * S, past + PAGE), topk_s)
        qpad = lambda a: jnp.pad(a.reshape(DB, S, LANE), ((0, 0), (0, QPAD - S), (0, 0)))
        thr, js = qpad(thr), qpad(js)
        q_s = jnp.pad(q[0].reshape(DB, S, ATT_W), ((0, 0), (0, QPAD - S), (0, 0)))
        o_att = _att_sample(page_table, q_s, keys, thr, js, new_page(kt), new_page(vt), ck, cv, l)[:, :S]

        def fix(prev, shift):
            p = jnp.zeros((DB, S) + prev.shape[2:], F32)
            for s in range(shift):
                p = p.at[:, s].set(prev[:, 2 - shift + s])
            return p.reshape((DB * S,) + prev.shape[2:])

        def chunked(a):
            return a.reshape(DB * S, n_ff, FF_CHUNK).transpose(1, 0, 2)

        sfix = (fix(state_sconv[l], 1), fix(state_sconv[l], 2),
                chunked(fix(state_ffn_conv[l], 1)), chunked(fix(state_ffn_conv[l], 2)))
        hs, u, a = _post(hs, o_gla.reshape(1, DB * S, GLA_W), scv, o_att.reshape(1, DB * S, ATT_W), *post_w,
                         sample_fix=sfix, n_tok=S)
        outs_s[0].append(k.reshape(DB, S, ATT_HEADS, HEAD_DIM))
        outs_s[1].append(v.reshape(DB, S, ATT_HEADS, HEAD_DIM))
        outs_s[2].append(ik.reshape(DB, S, IDX_DIM))
        outs_s[3].append(s_new.reshape(DB, GLA_HEADS, GLA_DK, GLA_DV))
        outs_s[4].append(u[0].reshape(DB, S, SCONV_W)[:, S - 2:])
        outs_s[5].append(_unchunk(a).reshape(DB, S, d_ff)[:, S - 2:])

    fw = final_norm[None]
    y_prompt = _final_norm(hp, fw, pad + N_META, S_len)
    y_sample = _final_norm(hs, fw, 0, DB * S).reshape(DB, S, D)
    sp = [jnp.stack(a, axis=0) for a in outs_p]
    ss = [jnp.stack(a, axis=0) for a in outs_s]
    return (y_prompt, y_sample, *sp, *ss)
```

```python
import functools

import jax
import jax.numpy as jnp
from jax import lax
from jax.experimental import pallas as pl
from jax.experimental.pallas import tpu as pltpu

F32 = jnp.float32
BF16 = jnp.bfloat16
I32 = jnp.int32

N_META = 16
GLA_HEADS, GLA_DK, GLA_DV = 4, 32, 64
GLA_RANK = 16
GLA_TAU = 16.0
GLA_W = GLA_HEADS * GLA_DV
GLA_QK = GLA_HEADS * GLA_DK
SCONV_W = 256
ATT_HEADS, HEAD_DIM = 8, 64
ATT_W = ATT_HEADS * HEAD_DIM
ROT_DIM = 16
ROPE_THETA = 500000.0
IDX_HEADS, IDX_DIM = 4, 64
IDX_W = IDX_HEADS * IDX_DIM
TOPK_MAX = 256
PAGE = 128
EPS = 1e-6
FF_CHUNK = 256

LANE = 128
QB = 128
GLA_CHUNK = 64
NEG_BIG = -1e30
VMEM_LIMIT = 56 * 1024 * 1024

SEG_GLA = (0, 768)
SEG_SCV = (768, 1536)
SEG_AQ = (1536, 2048)
SEG_AK = (2048, 2560)
SEG_IQ = (2560, 2816)
SEG_MISC = (2816, 2944)
N_PROJ = 2944
MISC_GLR = 64
MISC_IW = 80
TSEG_K = (0, 512)
TSEG_V = (512, 1024)
TSEG_IK = (1024, 1088)
TSEG_Q = (1088, 1600)
TSEG_IQ = (1600, 1856)
TSEG_IW = (1856, 1864)
N_PROJ_T = 1864

INT_MIN = -2147483648
NEG_INF_KEY = -2139095041


def _tile(n, cands=(512, 384, 256, 128, 64, 32, 16, 8)):
    for c in cands:
        if n % c == 0:
            return c
    raise ValueError(f"no tile for {n}")


def _dot(a, b):
    return jnp.dot(a, b, preferred_element_type=F32)


def _dot_nt(a, b):
    return lax.dot_general(a, b, (((1,), (1,)), ((), ())), preferred_element_type=F32)


def _dot_tn(a, b):
    return lax.dot_general(a, b, (((0,), (0,)), ((), ())), preferred_element_type=F32)


def _cparams(n_axes):
    return pltpu.CompilerParams(dimension_semantics=("arbitrary",) * n_axes, vmem_limit_bytes=VMEM_LIMIT)


def _const_spec(shape):
    nd = len(shape)
    return pl.BlockSpec(shape, lambda *_: (0,) * nd, pipeline_mode=pl.Buffered(1))


def _to_key(x):
    bits = pltpu.bitcast(x, I32)
    return bits ^ ((bits >> 31) & 0x7FFFFFFF)


def _proj_kernel(x_ref, nw_ref, w_ref, wt_ref, wg_ref, bg_ref, rope_ref, ropet_ref,
                 gla_ref, g_ref, scv_ref, q_ref, iq_ref, misc_ref, kb_ref, ikb_ref,
                 kt_ref, vt_ref, ikt_ref, vtb_ref, qtb_ref, iqtb_ref, iwt_ref):
    x = x_ref[...]
    xn = x * lax.rsqrt(jnp.mean(x * x, axis=-1, keepdims=True) + EPS) * nw_ref[...]
    xb = xn.astype(BF16)

    def seg(s):
        return _dot(xb, w_ref[:, s[0]:s[1]])

    def seg_t(s):
        return _dot_nt(wt_ref[s[0]:s[1], :], xb)

    cos, s1, s2 = rope_ref[0], rope_ref[1], rope_ref[2]

    def rope128(z, c, a, b):
        return z * c + pltpu.roll(z, LANE - ROT_DIM // 2, 1) * a + pltpu.roll(z, ROT_DIM // 2, 1) * b

    gla_ref[...] = seg(SEG_GLA)
    scv_ref[...] = seg(SEG_SCV)

    zq = seg(SEG_AQ)
    zk = seg(SEG_AK)
    for c in range(ATT_W // LANE):
        sl = slice(c * LANE, (c + 1) * LANE)
        q_ref[:, sl] = (rope128(zq[:, sl], cos, s1, s2) * HEAD_DIM ** -0.5).astype(BF16)
        kb_ref[:, sl] = rope128(zk[:, sl], cos, s1, s2).astype(BF16)
    zi = seg(SEG_IQ)
    for c in range(IDX_W // LANE):
        sl = slice(c * LANE, (c + 1) * LANE)
        iq_ref[:, sl] = (rope128(zi[:, sl], cos, s1, s2) * IDX_DIM ** -0.5).astype(BF16)

    zm = seg(SEG_MISC)
    gpre = jnp.dot(zm, wg_ref[...], preferred_element_type=F32, precision=lax.Precision.HIGHEST) + bg_ref[...]
    g_ref[...] = jax.nn.log_sigmoid(gpre) * (1.0 / GLA_TAU)
    lane = lax.broadcasted_iota(I32, zm.shape, 1)
    is_ik = lane < IDX_DIM
    mr = rope128(zm, jnp.where(is_ik, cos, 1.0), jnp.where(is_ik, s1, 0.0), jnp.where(is_ik, s2, 0.0))
    misc_ref[...] = mr * jnp.where((lane >= MISC_IW) & (lane < MISC_IW + IDX_HEADS), IDX_HEADS ** -0.5, 1.0)
    ikb_ref[...] = mr[:, :IDX_DIM].astype(BF16)

    cos_t, sin_t = ropet_ref[0], ropet_ref[1]
    half = ROT_DIM // 2

    def store_roped_t(z, n_heads, ref, scale=1.0):
        for h in range(n_heads):
            r0 = h * HEAD_DIM
            x1, x2 = z[r0:r0 + half], z[r0 + half:r0 + 2 * half]
            hd = jnp.concatenate([x1 * cos_t - x2 * sin_t, x2 * cos_t + x1 * sin_t,
                                  z[r0 + 2 * half:r0 + HEAD_DIM]], axis=0)
            ref[r0:r0 + HEAD_DIM, :] = (hd * scale).astype(ref.dtype)

    store_roped_t(seg_t(TSEG_K), ATT_HEADS, kt_ref)
    zv = seg_t(TSEG_V)
    vt_ref[...] = zv
    vtb_ref[...] = zv.astype(BF16)
    store_roped_t(seg_t(TSEG_IK), 1, ikt_ref)
    store_roped_t(seg_t(TSEG_Q), ATT_HEADS, qtb_ref, HEAD_DIM ** -0.5)
    store_roped_t(seg_t(TSEG_IQ), IDX_HEADS, iqtb_ref, IDX_DIM ** -0.5)
    iwt_ref[...] = seg_t(TSEG_IW) * IDX_HEADS ** -0.5


def _proj(x, nw, w, wt, wg, bg, rope, rope_t):
    B, R, D = x.shape
    tm = _tile(R)
    grid = (B, R // tm)

    def rspec(width):
        return pl.BlockSpec((None, tm, width), lambda b, t: (b, t, 0))

    def tspec(rows):
        return pl.BlockSpec((None, rows, tm), lambda b, t: (b, 0, t))

    outs = [(768, F32), (GLA_QK, F32), (768, F32), (ATT_W, BF16), (IDX_W, BF16), (LANE, F32), (ATT_W, BF16),
            (IDX_DIM, BF16)]
    outs_t = [(ATT_W, F32), (ATT_W, F32), (IDX_DIM, F32), (ATT_W, BF16), (ATT_W, BF16), (IDX_W, BF16),
              (2 * IDX_HEADS, F32)]
    return pl.pallas_call(
        _proj_kernel,
        grid=grid,
        in_specs=[rspec(D), _const_spec((1, D)), _const_spec((D, N_PROJ)), _const_spec((N_PROJ_T, D)),
                  _const_spec((LANE, GLA_QK)), _const_spec((1, GLA_QK)),
                  pl.BlockSpec((3, tm, LANE), lambda b, t: (0, t, 0)),
                  pl.BlockSpec((2, ROT_DIM // 2, tm), lambda b, t: (0, 0, t))],
        out_specs=[rspec(wd) for wd, _ in outs] + [tspec(r) for r, _ in outs_t],
        out_shape=[jax.ShapeDtypeStruct((B, R, wd), dt) for wd, dt in outs]
        + [jax.ShapeDtypeStruct((B, r, R), dt) for r, dt in outs_t],
        compiler_params=_cparams(2),
        name="proj",
    )(x, nw, w, wt, wg, bg, rope, rope_t)


def _gla_kernel(gla_ref, g_ref, gn_ref, o_ref, st_ref, s_sc):
    t = pl.program_id(1)
    C = GLA_CHUNK

    @pl.when(t == 0)
    def _():
        s_sc[...] = jnp.zeros_like(s_sc)

    r_i = lax.broadcasted_iota(I32, (C, C), 0)
    c_i = lax.broadcasted_iota(I32, (C, C), 1)
    causal = c_i <= r_i
    tri = jnp.where(causal, 1.0, 0.0).astype(F32)
    gn = gn_ref[...]
    for bi in range(gla_ref.shape[0]):
        z = gla_ref[bi]
        b = jnp.dot(tri, g_ref[bi], preferred_element_type=F32, precision=lax.Precision.HIGHEST)
        b_last = b[C - 1:C, :]
        b_mid = b[C // 2:C // 2 + 1, :]
        e_in = jnp.exp(b)
        e_q = jnp.exp(b - b_mid)
        e_k = jnp.exp(b_mid - b)
        e_out = jnp.exp(b_last - b)
        e_last = jnp.exp(b_last)
        qz = z[:, 0:GLA_QK] * GLA_DK ** -0.5
        kz = z[:, GLA_QK:2 * GLA_QK]
        q_in = (qz * e_in).astype(BF16)
        q_x = (qz * e_q).astype(BF16)
        k_x = (kz * e_k).astype(BF16)
        k_out = (kz * e_out).astype(BF16)
        for h in range(GLA_HEADS):
            ks = slice(h * GLA_DK, (h + 1) * GLA_DK)
            v = z[:, 2 * GLA_QK + h * GLA_DV:2 * GLA_QK + (h + 1) * GLA_DV]
            gr = z[:, 2 * GLA_QK + GLA_W + h * GLA_DV:2 * GLA_QK + GLA_W + (h + 1) * GLA_DV]
            vb = v.astype(BF16)
            st = s_sc[bi, h]
            att = jnp.where(causal, _dot_nt(q_x[:, ks], k_x[:, ks]), 0.0)
            o = _dot(att.astype(BF16), vb) + _dot_nt(q_in[:, ks], st.astype(BF16))
            s_sc[bi, h] = st * e_last[:, ks] + _dot_tn(vb, k_out[:, ks])
            on = o * lax.rsqrt(jnp.mean(o * o, axis=-1, keepdims=True) + EPS) * gn
            o_ref[bi, :, h * GLA_DV:(h + 1) * GLA_DV] = on * jax.nn.silu(gr)
    st_ref[...] = s_sc[...]


def _gla_prompt(gla, g, gn):
    B, R, _ = gla.shape
    C = GLA_CHUNK
    nb = _tile(B, (4, 2, 1))
    return pl.pallas_call(
        _gla_kernel,
        grid=(B // nb, R // C),
        in_specs=[pl.BlockSpec((nb, C, 768), lambda b, t: (b, t, 0)),
                  pl.BlockSpec((nb, C, GLA_QK), lambda b, t: (b, t, 0)),
                  _const_spec((1, GLA_DV))],
        out_specs=[pl.BlockSpec((nb, C, GLA_W), lambda b, t: (b, t, 0)),
                   pl.BlockSpec((nb, GLA_HEADS, GLA_DV, GLA_DK), lambda b, t: (b, 0, 0, 0))],
        out_shape=[jax.ShapeDtypeStruct((B, R, GLA_W), F32),
                   jax.ShapeDtypeStruct((B, GLA_HEADS, GLA_DV, GLA_DK), F32)],
        scratch_shapes=[pltpu.VMEM((nb, GLA_HEADS, GLA_DV, GLA_DK), F32)],
        compiler_params=_cparams(2),
        name="gla_prompt",
    )(gla, g, gn)


def _gla_step_kernel(qT_ref, kT_ref, gT_ref, vx_ref, gr_ref, gn_ref, s0_ref, o_ref, s_ref, *, n_tok):
    S = s0_ref[...]
    qT, kT, eT = qT_ref[...] * GLA_DK ** -0.5, kT_ref[...], jnp.exp(gT_ref[...])
    gn = gn_ref[...]
    gr = gr_ref[...]
    for t in range(n_tok):
        S = eT[:, t:t + 1] * S + kT[:, t:t + 1] * vx_ref[t]
        qs = qT[:, t:t + 1] * S
        for h in range(GLA_HEADS):
            o = jnp.sum(qs[h * GLA_DK:(h + 1) * GLA_DK, :], axis=0, keepdims=True)
            on = o * lax.rsqrt(jnp.mean(o * o, axis=-1, keepdims=True) + EPS) * gn
            hs = slice(h * GLA_DV, (h + 1) * GLA_DV)
            o_ref[t:t + 1, hs] = on * jax.nn.silu(gr[t:t + 1, hs])
    s_ref[...] = S


def _gla_sample(qT, kT, eT, vx, gr, gn, s0):
    DB, HK, n_tok = qT.shape

    def bspec(*shape):
        nd = len(shape)
        return pl.BlockSpec((None,) + shape, lambda b: (b,) + (0,) * nd)

    return pl.pallas_call(
        functools.partial(_gla_step_kernel, n_tok=n_tok),
        grid=(DB,),
        in_specs=[bspec(HK, n_tok), bspec(HK, n_tok), bspec(HK, n_tok), bspec(n_tok, HK, GLA_DV),
                  bspec(n_tok, GLA_W), _const_spec((1, GLA_DV)), bspec(HK, GLA_DV)],
        out_specs=[bspec(n_tok, GLA_W), bspec(HK, GLA_DV)],
        out_shape=[jax.ShapeDtypeStruct((DB, n_tok, GLA_W), F32), jax.ShapeDtypeStruct((DB, HK, GLA_DV), F32)],
        compiler_params=_cparams(1),
        name="gla_sample",
    )(qT, kT, eT, vx, gr, gn, s0)


def _chunk_loop(n_chunks, body, init):
    return body(0, init) if isinstance(n_chunks, int) and n_chunks == 1 else lax.fori_loop(0, n_chunks, body, init)


def _count_cols(key_ref, rsl, rows, n_chunks, cw, pred):
    def body(c, acc):
        off = pl.multiple_of(c * cw, cw)
        col = off + lax.broadcasted_iota(I32, (rows, LANE), 1)
        for j in range(cw // LANE):
            k = key_ref[rsl, pl.ds(off + j * LANE, LANE)]
            acc = acc + jnp.where(pred(k, col + j * LANE), 1, 0)
        return acc

    return _chunk_loop(n_chunks, body, jnp.zeros((rows, LANE), I32))


def _selected(key, col, thr, js):
    return (key - jnp.where(col >= js, 1, 0)) >= thr


def _select_threshold(key_ref, thr_ref, js_ref, n_groups, rows, n_chunks, cw, topk, idx_bits):
    gs = range(n_groups)
    rsl = [slice(g * rows, (g + 1) * rows) for g in gs]

    def partial_count(g, pred):
        return _count_cols(key_ref, rsl[g], rows, n_chunks, cw, pred)

    def count(g, pred):
        return jnp.sum(partial_count(g, pred), axis=1, keepdims=True)

    def start(g):
        return jnp.where(count(g, lambda k, col: k >= 0) >= topk, 0, INT_MIN).astype(I32)

    def bs(it, los):
        bit = lax.shift_left(jnp.int32(1), 30 - it)
        cands = [los[g] + bit for g in gs]
        parts = [partial_count(g, lambda k, col, cand=cands[g]: k >= cand) for g in gs]
        return tuple(jnp.where(jnp.sum(parts[g], axis=1, keepdims=True) >= topk, cands[g], los[g]) for g in gs)

    thrs = lax.fori_loop(0, 31, bs, tuple(start(g) for g in gs))

    for g in gs:
        thr = thrs[g]
        n_gt = count(g, lambda k, col: k > thr)
        n_ge = count(g, lambda k, col: k >= thr)
        need = topk - n_gt
        excess = (n_ge > topk) & (thr > NEG_INF_KEY)
        base = jnp.where(thr > NEG_INF_KEY, 1 << idx_bits, 0).astype(I32)
        thr_ref[rsl[g], :] = jnp.broadcast_to(thr, (rows, LANE))
        js_ref[rsl[g], :] = jnp.broadcast_to(base, (rows, LANE))

        @pl.when(jnp.max(jnp.where(excess, 1, 0)) > 0)
        def _(g=g, thr=thr, need=need, excess=excess, base=base):
            def bs2(it, j):
                cand = j + lax.shift_left(jnp.int32(1), idx_bits - 1 - it)
                c = count(g, lambda k, col: (k == thr) & (col < cand))
                return jnp.where(c < need, cand, j)

            j = lax.fori_loop(0, idx_bits, bs2, jnp.zeros((rows, 1), I32))
            js_ref[rsl[g], :] = jnp.broadcast_to(jnp.where(excess, j + 1, base), (rows, LANE))


SUB = 8


def _fold_rows(x, op):
    return op(x.reshape(x.shape[0] // SUB, SUB, x.shape[1]), axis=0)


def _count_keys(key_ref, n_chunks, kc, pred):
    qb = key_ref.shape[1]

    def body(c, acc):
        off = pl.multiple_of(c * kc, kc)
        kidx = off + lax.broadcasted_iota(I32, (kc, qb), 0)
        return acc + _fold_rows(jnp.where(pred(key_ref[pl.ds(off, kc), :], kidx), 1, 0), jnp.sum)

    acc = lax.fori_loop(0, n_chunks, body, jnp.zeros((SUB, qb), I32))
    return jnp.sum(acc, axis=0, keepdims=True)


def _select_threshold_t(key_ref, js_ref, n_chunks, kc, topk, idx_bits):
    qb = key_ref.shape[1]

    def count(pred):
        return _count_keys(key_ref, n_chunks, kc, pred)

    lo = jnp.where(count(lambda k, kidx: k >= 0) >= topk, 0, INT_MIN).astype(I32)

    def bs(it, lo):
        cand = lo + lax.shift_left(jnp.int32(1), 30 - it)
        return jnp.where(count(lambda k, kidx: k >= cand) >= topk, cand, lo)

    thr = lax.fori_loop(0, 31, bs, lo)
    n_gt = count(lambda k, kidx: k > thr)
    n_ge = count(lambda k, kidx: k >= thr)
    need = topk - n_gt
    excess = (n_ge > topk) & (thr > NEG_INF_KEY)
    base = jnp.where(thr > NEG_INF_KEY, 1 << idx_bits, 0).astype(I32)
    js_ref[...] = jnp.broadcast_to(base, js_ref.shape)

    @pl.when(jnp.max(jnp.where(excess, 1, 0)) > 0)
    def _():
        def bs2(it, j):
            cand = j + lax.shift_left(jnp.int32(1), idx_bits - 1 - it)
            c = count(lambda k, kidx: (k == thr) & (kidx < cand))
            return jnp.where(c < need, cand, j)

        j = lax.fori_loop(0, idx_bits, bs2, jnp.zeros((1, qb), I32))
        js_ref[...] = jnp.broadcast_to(jnp.where(excess, j + 1, base), js_ref.shape)

    return thr


def _dsa_prompt_kernel(iqt_ref, iwt_ref, qt_ref, ik_ref, k_ref, vt_ref, o_ref,
                       key_sc, js_sc, m_sc, lp_sc, acc_sc, *, pad, topk, qb, idx_bits):
    i = pl.program_id(1)
    kc = qb
    n_kc = i + 1
    qidx = i * qb + lax.broadcasted_iota(I32, (1, qb), 1)

    def rows(c):
        return pl.ds(pl.multiple_of(c * kc, kc), kc)

    def key_index(c):
        return c * kc + lax.broadcasted_iota(I32, (kc, qb), 0)

    iqt = iqt_ref[...]
    iwt = iwt_ref[...]

    def p1(c, carry):
        ik = ik_ref[rows(c), :]
        sc = iwt[0:1] * jnp.maximum(_dot(ik, iqt[0:IDX_DIM]), 0.0)
        for h in range(1, IDX_HEADS):
            sc = sc + iwt[h:h + 1] * jnp.maximum(_dot(ik, iqt[h * IDX_DIM:(h + 1) * IDX_DIM]), 0.0)
        kidx = key_index(c)
        sc = jnp.where((kidx <= qidx) & (kidx >= pad), sc + 0.0, -jnp.inf)
        key_sc[rows(c), :] = _to_key(sc)
        return carry

    lax.fori_loop(0, n_kc, p1, 0)

    thr = _select_threshold_t(key_sc, js_sc, n_kc, kc, topk, idx_bits)
    js = js_sc[0:1, :]

    def pmask(c, carry):
        sel = _selected(key_sc[rows(c), :], key_index(c), thr, js)
        key_sc[rows(c), :] = pltpu.bitcast(jnp.where(sel, 0.0, NEG_BIG), I32)
        return carry

    lax.fori_loop(0, n_kc, pmask, 0)

    qt = qt_ref[...].astype(F32)
    in_low = lax.broadcasted_iota(I32, (LANE, qb), 0) < HEAD_DIM
    qtm = []
    for h in range(ATT_HEADS):
        pair = qt[(h // 2) * LANE:(h // 2 + 1) * LANE]
        qtm.append(jnp.where(in_low if h % 2 == 0 else ~in_low, pair, 0.0).astype(BF16))

    def scores(c, h):
        ps = slice((h // 2) * LANE, (h // 2 + 1) * LANE)
        return _dot(k_ref[rows(c), ps], qtm[h]) + pltpu.bitcast(key_sc[rows(c), :], F32)

    m_sc[...] = jnp.full(m_sc.shape, NEG_BIG, F32)

    def pa(c, carry):
        for h in range(ATT_HEADS):
            m_sc[h] = jnp.maximum(m_sc[h], _fold_rows(scores(c, h), jnp.max))
        return carry

    lax.fori_loop(0, n_kc, pa, 0)
    for h in range(ATT_HEADS):
        m_sc[h] = jnp.broadcast_to(jnp.max(m_sc[h], axis=0, keepdims=True), (SUB, qb))

    lp_sc[...] = jnp.zeros_like(lp_sc)
    acc_sc[...] = jnp.zeros_like(acc_sc)

    def pb(c, carry):
        for h in range(ATT_HEADS):
            hs = slice(h * HEAD_DIM, (h + 1) * HEAD_DIM)
            p = jnp.exp(scores(c, h) - m_sc[h][0:1])
            lp_sc[h] += _fold_rows(p, jnp.sum)
            acc_sc[hs, :] += _dot(vt_ref[hs, rows(c)], p.astype(BF16))
        return carry

    lax.fori_loop(0, n_kc, pb, 0)

    valid = qidx >= pad
    for h in range(ATT_HEADS):
        hs = slice(h * HEAD_DIM, (h + 1) * HEAD_DIM)
        l = jnp.sum(lp_sc[h], axis=0, keepdims=True)
        acc_sc[hs, :] = jnp.where(valid, acc_sc[hs, :] / l, 0.0)
    o_ref[...] = acc_sc[...].T


def _dsa_prompt(iqtb, iwt, qtb, ikb, kb, vtb, pad, topk):
    B, R, _ = kb.shape
    qb = _tile(R, (384, 256, 128))
    assert qb >= topk
    idx_bits = max(1, (R - 1).bit_length())

    def qspec(rows):
        return pl.BlockSpec((None, rows, qb), lambda b, i: (b, 0, i))

    return pl.pallas_call(
        functools.partial(_dsa_prompt_kernel, pad=pad, topk=topk, qb=qb, idx_bits=idx_bits),
        grid=(B, R // qb),
        in_specs=[qspec(IDX_W), qspec(2 * IDX_HEADS), qspec(ATT_W),
                  pl.BlockSpec((None, R, IDX_DIM), lambda b, i: (b, 0, 0)),
                  pl.BlockSpec((None, R, ATT_W), lambda b, i: (b, 0, 0)),
                  pl.BlockSpec((None, ATT_W, R), lambda b, i: (b, 0, 0))],
        out_specs=pl.BlockSpec((None, qb, ATT_W), lambda b, i: (b, i, 0)),
        out_shape=jax.ShapeDtypeStruct((B, R, ATT_W), F32),
        scratch_shapes=[pltpu.VMEM((R, qb), I32), pltpu.VMEM((SUB, qb), I32),
                        pltpu.VMEM((ATT_HEADS, SUB, qb), F32), pltpu.VMEM((ATT_HEADS, SUB, qb), F32),
                        pltpu.VMEM((ATT_W, qb), F32)],
        compiler_params=_cparams(2),
        name="dsa_prompt",
    )(iqtb, iwt, qtb, ikb, kb, vtb)


QPAD = 8


def _idx_sample_kernel(pt_ref, iq_ref, w_ref, iknt_ref, *rest, pg, n_pg, past):
    pages = rest[:pg]
    key_ref = rest[pg]
    g = pl.program_id(1)
    iq = iq_ref[...]
    w = w_ref[...]

    def score(ikt):
        r = jnp.maximum(_dot(iq, ikt), 0.0) * w
        s = r[0:QPAD]
        for h in range(1, IDX_HEADS):
            s = s + r[h * QPAD:(h + 1) * QPAD]
        return s + 0.0

    ikt = jnp.concatenate([p[...].astype(BF16) for p in pages], axis=1)
    off = pl.multiple_of(g * (pg * PAGE), pg * PAGE)
    key_ref[:, pl.ds(off, pg * PAGE)] = _to_key(score(ikt))

    @pl.when(g == n_pg - 1)
    def _():
        sn = score(iknt_ref[...].astype(BF16))
        qrow = lax.broadcasted_iota(I32, (QPAD, PAGE), 0)
        colr = lax.broadcasted_iota(I32, (QPAD, PAGE), 1)
        key_ref[:, past:past + PAGE] = _to_key(jnp.where(colr <= qrow, sn, -jnp.inf))


def _idx_sample(page_table, iq, w, iknt, cache_ikt, layer):
    DB, n_pages = page_table.shape
    past = n_pages * PAGE
    pg = 16 if n_pages % 16 == 0 else (4 if n_pages % 4 == 0 else 1)
    n_pg = n_pages // pg
    ncol = past + PAGE
    rows = IDX_HEADS * QPAD

    def page_spec(j):
        return pl.BlockSpec((None, None, IDX_DIM, PAGE), lambda b, g, pt: (layer, pt[b, g * pg + j], 0, 0))

    def bspec(*shape):
        nd = len(shape)
        return pl.BlockSpec((None,) + shape, lambda b, g, pt: (b,) + (0,) * nd)

    grid_spec = pltpu.PrefetchScalarGridSpec(
        num_scalar_prefetch=1,
        grid=(DB, n_pg),
        in_specs=[bspec(rows, IDX_DIM), bspec(rows, 1), bspec(IDX_DIM, PAGE)] + [page_spec(j) for j in range(pg)],
        out_specs=bspec(QPAD, ncol),
    )
    return pl.pallas_call(
        functools.partial(_idx_sample_kernel, pg=pg, n_pg=n_pg, past=past),
        grid_spec=grid_spec,
        out_shape=jax.ShapeDtypeStruct((DB, QPAD, ncol), I32),
        compiler_params=_cparams(2),
        name="idx_sample",
    )(page_table, iq, w, iknt, *([cache_ikt] * pg))


def _thr_sample_kernel(key_ref, thr_ref, js_ref, *, n_groups, n_chunks, cw, topk, idx_bits):
    _select_threshold(key_ref, thr_ref, js_ref, n_groups, key_ref.shape[0] // n_groups, n_chunks, cw, topk, idx_bits)


def _thr_sample(keys, topk):
    n_rows, ncol = keys.shape
    rb = _tile(n_rows, (64, 32, 16))
    n_groups = 2 if rb % 32 == 0 else 1
    n_slab = ncol // LANE
    per_chunk = max(d for d in range(1, 65) if n_slab % d == 0)
    return pl.pallas_call(
        functools.partial(_thr_sample_kernel, n_groups=n_groups, n_chunks=n_slab // per_chunk, cw=per_chunk * LANE,
                          topk=topk, idx_bits=max(1, (ncol - 1).bit_length())),
        grid=(n_rows // rb,),
        in_specs=[pl.BlockSpec((rb, ncol), lambda i: (i, 0))],
        out_specs=[pl.BlockSpec((rb, LANE), lambda i: (i, 0)), pl.BlockSpec((rb, LANE), lambda i: (i, 0))],
        out_shape=[jax.ShapeDtypeStruct((n_rows, LANE), I32), jax.ShapeDtypeStruct((n_rows, LANE), I32)],
        compiler_params=_cparams(1),
        name="thr_sample",
    )(keys)


def _att_sample_kernel(pt_ref, q_ref, keyp_ref, keyn_ref, thr_ref, js_ref, kn_ref, vn_ref, *rest, pg, n_pg, past):
    kpages = rest[:pg]
    vpages = rest[pg:2 * pg]
    o_ref = rest[2 * pg]
    m_sc, l_sc, acc_sc = rest[2 * pg + 1:]
    g = pl.program_id(1)
    rows = ATT_HEADS * QPAD

    @pl.when(g == 0)
    def _():
        m_sc[...] = jnp.full(m_sc.shape, NEG_BIG, F32)
        l_sc[...] = jnp.zeros_like(l_sc)
        acc_sc[...] = jnp.zeros_like(acc_sc)

    q = q_ref[...]
    r_h = lax.broadcasted_iota(I32, (rows, ATT_W), 0) >> 3
    l_h = lax.broadcasted_iota(I32, (rows, ATT_W), 1) >> 6
    own = r_h == l_h
    qbd = jnp.where(own, jnp.concatenate([q.astype(F32)] * ATT_HEADS, axis=0), 0.0).astype(BF16)
    thr = thr_ref[:, 0:1]
    js = js_ref[:, 0:1]

    def update(kts, vts, key, colbase):
        width = key.shape[1]
        col = colbase + lax.broadcasted_iota(I32, (QPAD, width), 1)
        bias8 = jnp.where(_selected(key, col, thr, js), 0.0, NEG_BIG)
        s = jnp.concatenate([_dot(qbd, kt.astype(BF16)) for kt in kts], axis=1)
        s = s + jnp.concatenate([bias8] * ATT_HEADS, axis=0)
        m_old = m_sc[...]
        m_new = jnp.maximum(m_old, jnp.max(s, axis=1, keepdims=True))
        alpha = jnp.exp(m_old - m_new)
        pe = jnp.exp(s - m_new[:, 0:1])
        l_sc[...] = alpha * l_sc[...] + jnp.sum(pe, axis=1, keepdims=True)
        m_sc[...] = m_new
        pv = _dot_nt(pe[:, 0:PAGE].astype(BF16), vts[0].astype(BF16))
        for j in range(1, len(vts)):
            pv = pv + _dot_nt(pe[:, j * PAGE:(j + 1) * PAGE].astype(BF16), vts[j].astype(BF16))
        acc_sc[...] = alpha[:, 0:1] * acc_sc[...] + pv

    update([p[...] for p in kpages], [p[...] for p in vpages], keyp_ref[...], g * (pg * PAGE))

    @pl.when(g == n_pg - 1)
    def _():
        update([kn_ref[...]], [vn_ref[...]], keyn_ref[...], past)
        res = jnp.where(own, acc_sc[...] / l_sc[:, 0:1], 0.0)
        o = res[0:QPAD]
        for h in range(1, ATT_HEADS):
            o = o + res[h * QPAD:(h + 1) * QPAD]
        o_ref[...] = o


def _att_sample(page_table, q, keys, thr, js, kn, vn, cache_k, cache_v, layer):
    DB, n_pages = page_table.shape
    past = n_pages * PAGE
    pg = 16 if n_pages % 16 == 0 else (4 if n_pages % 4 == 0 else 1)
    n_pg = n_pages // pg
    rows = ATT_HEADS * QPAD

    def kv_spec(j):
        return pl.BlockSpec((None, None, ATT_W, PAGE), lambda b, g, pt: (layer, pt[b, g * pg + j], 0, 0))

    def bspec(*shape):
        nd = len(shape)
        return pl.BlockSpec((None,) + shape, lambda b, g, pt: (b,) + (0,) * nd)

    grid_spec = pltpu.PrefetchScalarGridSpec(
        num_scalar_prefetch=1,
        grid=(DB, n_pg),
        in_specs=[bspec(QPAD, ATT_W),
                  pl.BlockSpec((None, QPAD, pg * PAGE), lambda b, g, pt: (b, 0, g)),
                  pl.BlockSpec((None, QPAD, PAGE), lambda b, g, pt: (b, 0, n_pages)),
                  bspec(QPAD, LANE), bspec(QPAD, LANE), bspec(ATT_W, PAGE), bspec(ATT_W, PAGE)]
        + [kv_spec(j) for j in range(pg)] + [kv_spec(j) for j in range(pg)],
        out_specs=bspec(QPAD, ATT_W),
        scratch_shapes=[pltpu.VMEM((rows, PAGE), F32), pltpu.VMEM((rows, PAGE), F32), pltpu.VMEM((rows, ATT_W), F32)],
    )
    return pl.pallas_call(
        functools.partial(_att_sample_kernel, pg=pg, n_pg=n_pg, past=past),
        grid_spec=grid_spec,
        out_shape=jax.ShapeDtypeStruct((DB, QPAD, ATT_W), F32),
        compiler_params=_cparams(2),
        name="att_sample",
    )(page_table, q, keys, keys, thr, js, kn, vn, *([cache_k] * pg), *([cache_v] * pg))


def _conv3(u, w, first, second):
    u1 = jnp.where(first[0], first[1], pltpu.roll(u, 1, 0))
    u2 = jnp.where(second[0], second[1], pltpu.roll(u, 2, 0))
    return w[0:1] * u2 + w[1:2] * u1 + w[2:3] * u


def _post_kernel(*refs, sample, n_tok, n_ff):
    if sample:
        (h_ref, og_ref, scv_ref, oa_ref, p1s_ref, p2s_ref, p1f_ref, p2f_ref, wsc_ref, wout_ref, fnw_ref, wua_ref,
         wub_ref, wfc_ref, wdn_ref, h2_ref, u_ref, a_ref, acc_sc, hn_sc) = refs
    else:
        (h_ref, og_ref, scv_ref, oa_ref, wsc_ref, wout_ref, fnw_ref, wua_ref, wub_ref, wfc_ref, wdn_ref,
         h2_ref, sts_ref, stf_ref, cs_sc, cf_sc, acc_sc, hn_sc) = refs
    tm = h_ref.shape[0]

    if not sample:
        @pl.when(pl.program_id(1) == 0)
        def _():
            cs_sc[...] = jnp.zeros_like(cs_sc)
            cf_sc[...] = jnp.zeros_like(cf_sc)

    def fixes(width, c0, c1, p1, p2):
        rows = lax.broadcasted_iota(I32, (tm, width), 0)
        if sample:
            sp = lax.rem(rows, n_tok)
            return (sp < 1, p1), (sp < 2, p2)
        return (rows == 0, c1), (rows < 2, jnp.where(rows == 0, c0, c1))

    scv = scv_ref[...]
    sb = scv[:, 0:SCONV_W]
    u = scv[:, SCONV_W:2 * SCONV_W] * scv[:, 2 * SCONV_W:3 * SCONV_W]
    if sample:
        f1, f2 = fixes(SCONV_W, None, None, p1s_ref[...], p2s_ref[...])
        u_ref[...] = u
    else:
        f1, f2 = fixes(SCONV_W, cs_sc[0:1, :], cs_sc[1:2, :], None, None)
        cs_sc[0:2, :] = u[tm - 2:tm, :]
        sts_ref[...] = u[tm - 2:tm, :]
    o_s = sb * _conv3(u, wsc_ref[...], f1, f2)

    mix = (_dot(og_ref[...].astype(BF16), wout_ref[0:GLA_W, :])
           + _dot(o_s.astype(BF16), wout_ref[GLA_W:GLA_W + SCONV_W, :])
           + _dot(oa_ref[...].astype(BF16), wout_ref[GLA_W + SCONV_W:, :]))
    h1 = h_ref[...] + mix
    hn_sc[...] = (h1 * lax.rsqrt(jnp.mean(h1 * h1, axis=-1, keepdims=True) + EPS) * fnw_ref[...]).astype(BF16)
    acc_sc[...] = h1

    def ff(c, carry):
        hn = hn_sc[...]
        a = _dot(hn, wua_ref[c])
        bgate = _dot(hn, wub_ref[c])
        if sample:
            f1, f2 = fixes(FF_CHUNK, None, None, p1f_ref[c], p2f_ref[c])
            a_ref[c] = a
        else:
            f1, f2 = fixes(FF_CHUNK, cf_sc[c, 0:1, :], cf_sc[c, 1:2, :], None, None)
            cf_sc[c, 0:2, :] = a[tm - 2:tm, :]
            stf_ref[c] = a[tm - 2:tm, :]
        ac = _conv3(a, wfc_ref[c], f1, f2)
        act = (jax.nn.silu(ac) * bgate).astype(BF16)
        acc_sc[...] += _dot(act, wdn_ref[c])
        return carry

    lax.fori_loop(0, n_ff, ff, 0)
    h2_ref[...] = acc_sc[...]


def _post(h, og, scv, oa, wsc, wout, fnw, wua, wub, wfc, wdn, sample_fix=None, n_tok=1):
    B, R, D = h.shape
    n_ff = wua.shape[0]
    sample = sample_fix is not None
    tm = _tile(R, (704, 512, 384, 256, 128, 64, 32, 16, 8))

    def rspec(width):
        return pl.BlockSpec((None, tm, width), lambda b, t: (b, t, 0))

    in_specs = [rspec(D), rspec(GLA_W), rspec(768), rspec(ATT_W)]
    args = [h, og, scv, oa]
    if sample:
        assert B == 1 and R == tm
        p1s, p2s, p1f, p2f = sample_fix
        in_specs += [_const_spec((tm, SCONV_W)), _const_spec((tm, SCONV_W)),
                     _const_spec((n_ff, tm, FF_CHUNK)), _const_spec((n_ff, tm, FF_CHUNK))]
        args += [p1s, p2s, p1f, p2f]
        out_specs = [rspec(D), rspec(SCONV_W), pl.BlockSpec((n_ff, tm, FF_CHUNK), lambda b, t: (0, 0, 0))]
        out_shape = [jax.ShapeDtypeStruct((B, R, D), F32), jax.ShapeDtypeStruct((B, R, SCONV_W), F32),
                     jax.ShapeDtypeStruct((n_ff, R, FF_CHUNK), F32)]
        scratch = []
    else:
        out_specs = [rspec(D), pl.BlockSpec((None, 2, SCONV_W), lambda b, t: (b, 0, 0)),
                     pl.BlockSpec((None, n_ff, 2, FF_CHUNK), lambda b, t: (b, 0, 0, 0))]
        out_shape = [jax.ShapeDtypeStruct((B, R, D), F32), jax.ShapeDtypeStruct((B, 2, SCONV_W), F32),
                     jax.ShapeDtypeStruct((B, n_ff, 2, FF_CHUNK), F32)]
        scratch = [pltpu.VMEM((8, SCONV_W), F32), pltpu.VMEM((n_ff, 8, FF_CHUNK), F32)]
    in_specs += [_const_spec((3, SCONV_W)), _const_spec((D, D)), _const_spec((1, D)),
                 _const_spec((n_ff, D, FF_CHUNK)), _const_spec((n_ff, D, FF_CHUNK)),
                 _const_spec((n_ff, 3, FF_CHUNK)), _const_spec((n_ff, FF_CHUNK, D))]
    args += [wsc, wout, fnw, wua, wub, wfc, wdn]
    scratch += [pltpu.VMEM((tm, D), F32), pltpu.VMEM((tm, D), BF16)]
    return pl.pallas_call(
        functools.partial(_post_kernel, sample=sample, n_tok=n_tok, n_ff=n_ff),
        grid=(B, R // tm),
        in_specs=in_specs,
        out_specs=out_specs,
        out_shape=out_shape,
        scratch_shapes=scratch,
        compiler_params=_cparams(2),
        name="post_sample" if sample else "post_prompt",
    )(*args)


def _norm_kernel(x_ref, w_ref, o_ref):
    x = x_ref[...]
    o_ref[...] = x * lax.rsqrt(jnp.mean(x * x, axis=-1, keepdims=True) + EPS) * w_ref[...]


def _final_norm(h, w, skip_rows, n_rows):
    B, _, D = h.shape
    tm = LANE if n_rows % LANE == 0 else n_rows
    assert skip_rows % tm == 0 and n_rows % tm == 0
    off = skip_rows // tm
    return pl.pallas_call(
        _norm_kernel,
        grid=(B, n_rows // tm),
        in_specs=[pl.BlockSpec((None, tm, D), lambda b, t: (b, t + off, 0)), _const_spec((1, D))],
        out_specs=pl.BlockSpec((None, tm, D), lambda b, t: (b, t, 0)),
        out_shape=jax.ShapeDtypeStruct((B, n_rows, D), F32),
        compiler_params=_cparams(2),
        name="final_norm",
    )(h, w)


def _rope_tables(pos):
    half = ROT_DIM // 2
    inv_freq = ROPE_THETA ** (-jnp.arange(half, dtype=F32) / half)
    ang = pos.astype(F32)[:, None] * inv_freq[None, :]
    cos, sin = jnp.cos(ang), jnp.sin(ang)
    n = pos.shape[0]
    one = jnp.ones((n, HEAD_DIM - ROT_DIM), F32)
    zero = jnp.zeros((n, HEAD_DIM - ROT_DIM), F32)
    z8 = jnp.zeros((n, half), F32)
    c64 = jnp.concatenate([cos, cos, one], axis=1)
    a64 = jnp.concatenate([-sin, z8, zero], axis=1)
    b64 = jnp.concatenate([z8, sin, zero], axis=1)
    row_major = jnp.stack([jnp.tile(t, (1, LANE // HEAD_DIM)) for t in (c64, a64, b64)], axis=0)
    return row_major, jnp.stack([cos.T, sin.T], axis=0)


def _layer_weights(l, attn_norm, w_in, w_gate_up, b_gate, gla_norm, w_sconv, w_out, ffn_norm, w_up, w_ffn_conv, w_down):
    D = w_in.shape[1]
    wi = w_in[l]
    cuts = [0, 128, 256, 512, 528, 784, 1040, 1296, 1552, 2064, 2576, 3088, 3344, 3408, 3412]
    gq, gk, gv, glr, gr, sb, sc, sx, aq, ak, av, iq, ik, iw = [wi[:, cuts[i]:cuts[i + 1]] for i in range(14)]
    pad = jnp.zeros((D, LANE - IDX_DIM - GLA_RANK - IDX_HEADS), F32)
    wp = jnp.concatenate([gq, gk, gv, gr, sb, sc, sx, aq, ak, iq, ik, glr, iw, pad], axis=1).astype(BF16)
    wt = jnp.concatenate([ak, av, ik, aq, iq, iw, jnp.zeros((D, IDX_HEADS), F32)], axis=1).T.astype(BF16)
    wg = jnp.zeros((LANE, GLA_QK), F32).at[MISC_GLR:MISC_GLR + GLA_RANK].set(w_gate_up[l])
    d_ff = w_down.shape[1]
    n_ff = d_ff // FF_CHUNK
    wu = w_up[l].astype(BF16)
    wua = wu[:, :d_ff].reshape(D, n_ff, FF_CHUNK).transpose(1, 0, 2)
    wub = wu[:, d_ff:].reshape(D, n_ff, FF_CHUNK).transpose(1, 0, 2)
    wfc = w_ffn_conv[l].reshape(3, n_ff, FF_CHUNK).transpose(1, 0, 2)
    wdn = w_down[l].astype(BF16).reshape(n_ff, FF_CHUNK, D)
    return dict(nw=attn_norm[l][None], wp=wp, wt=wt, wg=wg, bg=b_gate[l][None], gn=gla_norm[l][None], wsc=w_sconv[l],
                wout=w_out[l].astype(BF16), fnw=ffn_norm[l][None], wua=wua, wub=wub, wfc=wfc, wdn=wdn)


def _unchunk(a):
    a = jnp.swapaxes(a, -3, -2)
    return a.reshape(a.shape[:-2] + (a.shape[-2] * a.shape[-1],))


def kernel(x_prompt, x_sample, cache_k, cache_v, cache_idx_k, page_table, state_gla, state_sconv, state_ffn_conv,
           meta_tokens, attn_norm, w_in, w_gate_up, b_gate, gla_norm, w_sconv, w_out, ffn_norm, w_up, w_ffn_conv,
           w_down, final_norm):
    B, S_len, D = x_prompt.shape
    depth = w_in.shape[0]
    T = S_len + N_META
    pad = (-T) % LANE
    R = pad + T
    DB, S = x_sample.shape[:2]
    n_pages = page_table.shape[1]
    past = n_pages * PAGE
    topk_p = min(TOPK_MAX, T // 4)
    topk_s = min(TOPK_MAX, (past + S) // 4)
    d_ff = w_down.shape[1]
    n_ff = d_ff // FF_CHUNK
    n_pool = cache_k.shape[1]
    assert S <= QPAD and (DB * S) % 8 == 0

    hp = jnp.concatenate([jnp.zeros((B, pad, D), F32), jnp.broadcast_to(meta_tokens[None], (B, N_META, D)), x_prompt],
                         axis=1)
    hs = x_sample.reshape(1, DB * S, D)
    rope_p, rope_pt = _rope_tables(jnp.maximum(jnp.arange(R) - pad, 0))
    rope_s, rope_st = _rope_tables(past + jnp.arange(DB * S) % S)
    ck = jnp.transpose(cache_k, (0, 1, 3, 4, 2)).reshape(depth, n_pool, ATT_W, PAGE)
    cv = jnp.transpose(cache_v, (0, 1, 3, 4, 2)).reshape(depth, n_pool, ATT_W, PAGE)
    cik = jnp.transpose(cache_idx_k, (0, 1, 3, 2))

    outs_p = [[] for _ in range(6)]
    outs_s = [[] for _ in range(6)]
    for l in range(depth):
        W = _layer_weights(l, attn_norm, w_in, w_gate_up, b_gate, gla_norm, w_sconv, w_out, ffn_norm, w_up,
                           w_ffn_conv, w_down)
        post_w = (W["wsc"], W["wout"], W["fnw"], W["wua"], W["wub"], W["wfc"], W["wdn"])

        proj_w = (W["nw"], W["wp"], W["wt"], W["wg"], W["bg"])
        gla, g, scv, _, _, _, kb, ikb, kt, vt, ikt, vtb, qtb, iqtb, iwt = _proj(hp, *proj_w, rope_p, rope_pt)
        o_gla, st_t = _gla_prompt(gla, g, W["gn"])
        o_att = _dsa_prompt(iqtb, iwt, qtb, ikb, kb, vtb, pad, topk_p)
        hp, st_s, st_f = _post(hp, o_gla, scv, o_att, *post_w)

        def heads_last(a):
            return a[:, :, pad:].reshape(B, ATT_HEADS, HEAD_DIM, T).transpose(0, 3, 1, 2)

        outs_p[0].append(heads_last(kt))
        outs_p[1].append(heads_last(vt))
        outs_p[2].append(ikt[:, :, pad:].transpose(0, 2, 1))
        outs_p[3].append(jnp.swapaxes(st_t, -1, -2))
        outs_p[4].append(st_s)
        outs_p[5].append(_unchunk(st_f))

        gla, g, scv, q, iq, misc, _, _, kt, vt, ikt, _, _, _, _ = _proj(hs, *proj_w, rope_s, rope_st)
        k, v, ik = kt[0].T, vt[0].T, ikt[0].T
        z = gla[0].reshape(DB, S, 768)
        tr = lambda a: jnp.swapaxes(a, 1, 2)
        eT = tr(g[0].reshape(DB, S, GLA_QK))
        qT = tr(z[:, :, 0:GLA_QK])
        kT = tr(z[:, :, GLA_QK:2 * GLA_QK])
        vx = jnp.repeat(z[:, :, 2 * GLA_QK:2 * GLA_QK + GLA_W].reshape(DB, S, GLA_HEADS, 1, GLA_DV), GLA_DK, axis=3)
        vx = vx.reshape(DB, S, GLA_QK, GLA_DV)
        gr = z[:, :, 2 * GLA_QK + GLA_W:]
        s0 = state_gla[l].reshape(DB, GLA_QK, GLA_DV)
        o_gla, s_new = _gla_sample(qT, kT, eT, vx, gr, W["gn"], s0)

        def hq(a, n_h, d):
            a = a.reshape(DB, S, n_h, d).transpose(0, 2, 1, 3)
            a = jnp.pad(a, ((0, 0), (0, 0), (0, QPAD - S), (0, 0)))
            return a.reshape(DB, n_h * QPAD, d)

        m0 = misc[0]
        iq_s = hq(iq[0], IDX_HEADS, IDX_DIM)
        w_s = hq(m0[:, MISC_IW:MISC_IW + IDX_HEADS], IDX_HEADS, 1)

        def new_page(a):
            a = a[0].reshape(a.shape[1], DB, S).transpose(1, 0, 2)
            return jnp.pad(a, ((0, 0), (0, 0), (0, PAGE - S)))

        keys = _idx_sample(page_table, iq_s, w_s, new_page(ikt), cik, l)
        thr, js = _thr_sample(keys[:, :S].reshape(DB * S, past + PAGE), topk_s)
        qpad = lambda a: jnp.pad(a.reshape(DB, S, LANE), ((0, 0), (0, QPAD - S), (0, 0)))
        thr, js = qpad(thr), qpad(js)
        q_s = jnp.pad(q[0].reshape(DB, S, ATT_W), ((0, 0), (0, QPAD - S), (0, 0)))
        o_att = _att_sample(page_table, q_s, keys, thr, js, new_page(kt), new_page(vt), ck, cv, l)[:, :S]

        def fix(prev, shift):
            p = jnp.zeros((DB, S) + prev.shape[2:], F32)
            for s in range(shift):
                p = p.at[:, s].set(prev[:, 2 - shift + s])
            return p.reshape((DB * S,) + prev.shape[2:])

        def chunked(a):
            return a.reshape(DB * S, n_ff, FF_CHUNK).transpose(1, 0, 2)

        sfix = (fix(state_sconv[l], 1), fix(state_sconv[l], 2),
                chunked(fix(state_ffn_conv[l], 1)), chunked(fix(state_ffn_conv[l], 2)))
        hs, u, a = _post(hs, o_gla.reshape(1, DB * S, GLA_W), scv, o_att.reshape(1, DB * S, ATT_W), *post_w,
                         sample_fix=sfix, n_tok=S)
        outs_s[0].append(k.reshape(DB, S, ATT_HEADS, HEAD_DIM))
        outs_s[1].append(v.reshape(DB, S, ATT_HEADS, HEAD_DIM))
        outs_s[2].append(ik.reshape(DB, S, IDX_DIM))
        outs_s[3].append(s_new.reshape(DB, GLA_HEADS, GLA_DK, GLA_DV))
        outs_s[4].append(u[0].reshape(DB, S, SCONV_W)[:, S - 2:])
        outs_s[5].append(_unchunk(a).reshape(DB, S, d_ff)[:, S - 2:])

    fw = final_norm[None]
    y_prompt = _final_norm(hp, fw, pad + N_META, S_len)
    y_sample = _final_norm(hs, fw, 0, DB * S).reshape(DB, S, D)
    sp = [jnp.stack(a, axis=0) for a in outs_p]
    ss = [jnp.stack(a, axis=0) for a in outs_s]
    return (y_prompt, y_sample, *sp, *ss)
```
